```python
import jax, jax.numpy as jnp
from jax import lax
import numpy as np

D_MODEL = 1024
BATCH = 8
SEQ = 2048
DEPTH = 4
DEC_BATCH = 128
DEC_SEQ = 8
PAST_LEN = 8192
PAGE_SIZE = 128

HEAD_DIM = 64
H_A = 8
KV_A = 2
WIN_A = 128
H_B = 8
DILATED_PAIRS = ((128, 1), (512, 4), (2048, 16))
W_MAX = 2048
BLOCK_Q = 128
D_FF = ((8 * D_MODEL // 3) + 127) // 128 * 128
N_EXPERTS = 8
TOP_K = 2
D_EXPERT = 7 * D_MODEL // 2
ALPHA = (2.0 * DEPTH) ** 0.25
BETA = (8.0 * DEPTH) ** -0.25
LN_EPS = 1e-5
NEG_INF = -1e30
SPLITS = (H_A * HEAD_DIM, KV_A * HEAD_DIM, KV_A * HEAD_DIM,
          H_B * HEAD_DIM, H_B * HEAD_DIM, H_B * HEAD_DIM, D_MODEL, D_MODEL)
IN_COLS = sum(SPLITS)
N_DENSE = (DEPTH + 1) // 2
N_MOE = DEPTH // 2

kernel_name = 'hybrid_swa_sink_dilated_gated_decoder_step'


def _layer_norm(x, g, b):
    xf = x.astype(jnp.float32)
    mu = jnp.mean(xf, axis=-1, keepdims=True)
    var = jnp.mean(jnp.square(xf - mu), axis=-1, keepdims=True)
    return ((xf - mu) * lax.rsqrt(var + LN_EPS) * g + b).astype(x.dtype)


def _alibi_slopes():
    n = H_A + H_B
    s = jnp.exp2(-8.0 * jnp.arange(1, n + 1, dtype=jnp.float32) / n)
    return s[0::2], s[1::2]


def _band_attention(q, k, v, q_pos, k_pos, max_dist, slopes, sink=None):
    bq, nq, h, dh = q.shape
    p = k.shape[1] - nq
    g = k.shape[2]
    r = h // g
    qb_size = min(BLOCK_Q, nq)
    nblk = -(-nq // qb_size)
    pad = nblk * qb_size - nq
    if pad:
        q = jnp.pad(q, ((0, 0), (0, pad), (0, 0), (0, 0)))
        k = jnp.pad(k, ((0, 0), (0, pad), (0, 0), (0, 0)))
        v = jnp.pad(v, ((0, 0), (0, pad), (0, 0), (0, 0)))
        q_pos = jnp.pad(q_pos, ((0, 0), (0, pad)))
        k_pos = jnp.pad(k_pos, ((0, 0), (0, pad)), constant_values=2 ** 30)
    qb = q.reshape(bq, nblk, qb_size, g, r, dh)
    qpb = q_pos.reshape(q_pos.shape[0], nblk, qb_size)
    idx = jnp.arange(nblk)[:, None] * qb_size + jnp.arange(qb_size + p)[None, :]
    kb = k[:, idx]
    vb = v[:, idx]
    kpb = k_pos[:, idx]
    s = jnp.einsum('bnqgrd,bnkgd->bngrqk', qb, kb).astype(jnp.float32) * (dh ** -0.5)
    dist = qpb[..., :, None] - kpb[..., None, :]
    valid = (kpb[..., None, :] >= 0) & (dist >= 0) & (dist <= max_dist)
    bias = -slopes.reshape(g, r)[:, :, None, None] * dist[:, :, None, None].astype(jnp.float32)
    s = jnp.where(valid[:, :, None, None], s + bias, NEG_INF)
    m = jnp.max(s, axis=-1, keepdims=True)
    if sink is not None:
        sk = sink.reshape(g, r)[:, :, None, None].astype(jnp.float32)
        m = jnp.maximum(m, sk)
    e = jnp.exp(s - m)
    l = jnp.sum(e, axis=-1, keepdims=True)
    if sink is not None:
        l = l + jnp.exp(sk - m)
    o = jnp.einsum('bngrqk,bnkgd->bnqgrd', e / l, vb.astype(jnp.float32))
    o = o.reshape(bq, nblk * qb_size, h, dh)[:, :nq]
    lse = jnp.transpose((m + jnp.log(l))[..., 0], (0, 1, 4, 2, 3)).reshape(bq, nblk * qb_size, h)[:, :nq]
    return o.astype(q.dtype), lse


def _dilated_pair(q, k_ext, v_ext, pos0, w, d, slopes):
    b, t, h, dh = q.shape
    lp = k_ext.shape[1] - t
    tp = -(-t // d) * d
    pad = tp - t
    k = jnp.pad(k_ext[:, lp - w:], ((0, 0), (0, pad), (0, 0), (0, 0)))
    v = jnp.pad(v_ext[:, lp - w:], ((0, 0), (0, pad), (0, 0), (0, 0)))
    qq = jnp.pad(q, ((0, 0), (0, pad), (0, 0), (0, 0)))
    nk, nq = (w + tp) // d, tp // d

    def to_sub(a, n):
        return a.reshape(b, n, d, h, dh).transpose(0, 2, 1, 3, 4).reshape(b * d, n, h, dh)

    k_pos = jnp.tile((pos0 - w + jnp.arange(w + tp, dtype=jnp.int32)).reshape(nk, d).T, (b, 1))
    q_pos = jnp.tile((pos0 + jnp.arange(tp, dtype=jnp.int32)).reshape(nq, d).T, (b, 1))
    o, lse = _band_attention(to_sub(qq, nq), to_sub(k, nk), to_sub(v, nk), q_pos, k_pos, w, slopes)
    o = o.reshape(b, d, nq, h, dh).transpose(0, 2, 1, 3, 4).reshape(b, tp, h, dh)[:, :t]
    lse = lse.reshape(b, d, nq, h).transpose(0, 2, 1, 3).reshape(b, tp, h)[:, :t]
    return o, lse


def _token_mixer(x, ka_prev, va_prev, kb_prev, vb_prev, pos0, keep_b,
                 w_in, sink, w_br_a, w_br_b, w_out):
    b, t, _ = x.shape
    offs = [int(o) for o in np.cumsum(SPLITS)[:-1]]
    q_a, k_a, v_a, q_b, k_b, v_b, g_a, g_b = jnp.split(x @ w_in, offs, axis=-1)
    q_a = q_a.reshape(b, t, H_A, HEAD_DIM)
    k_a = k_a.reshape(b, t, KV_A, HEAD_DIM)
    v_a = v_a.reshape(b, t, KV_A, HEAD_DIM)
    q_b = q_b.reshape(b, t, H_B, HEAD_DIM)
    k_b = k_b.reshape(b, t, H_B, HEAD_DIM)
    v_b = v_b.reshape(b, t, H_B, HEAD_DIM)
    slopes_a, slopes_b = _alibi_slopes()
    q_pos = (pos0 + jnp.arange(t, dtype=jnp.int32))[None]

    ka_ext = jnp.concatenate([ka_prev, k_a], axis=1)
    va_ext = jnp.concatenate([va_prev, v_a], axis=1)
    pa = ka_prev.shape[1]
    ka_pos = (pos0 - pa + jnp.arange(pa + t, dtype=jnp.int32))[None]
    o_a, _ = _band_attention(q_a, ka_ext, va_ext, q_pos, ka_pos, WIN_A - 1, slopes_a, sink)

    kb_ext = jnp.concatenate([kb_prev, k_b], axis=1)
    vb_ext = jnp.concatenate([vb_prev, v_b], axis=1)
    outs, lses = [], []
    for (w, d) in DILATED_PAIRS:
        o_i, l_i = _dilated_pair(q_b, kb_ext, vb_ext, pos0, w, d, slopes_b)
        outs.append(o_i)
        lses.append(l_i)
    wts = jax.nn.softmax(jnp.stack(lses, axis=0), axis=0)
    o_b = jnp.einsum('pbth,pbthd->bthd', wts, jnp.stack(outs, axis=0).astype(jnp.float32)).astype(x.dtype)

    merged = (jax.nn.sigmoid(g_a) * (o_a.reshape(b, t, -1) @ w_br_a)
              + jax.nn.sigmoid(g_b) * (o_b.reshape(b, t, -1) @ w_br_b))
    y = merged @ w_out
    return (y, ka_ext[:, -WIN_A:], va_ext[:, -WIN_A:], kb_ext[:, -keep_b:], vb_ext[:, -keep_b:])


def _swiglu(x, w_gate, w_up, w_down):
    return (jax.nn.silu(x @ w_gate) * (x @ w_up)) @ w_down


def _moe_swiglu(x, w_router, w_gate, w_up, w_down):
    logits = jnp.einsum('btd,de->bte', x.astype(jnp.float32), w_router.astype(jnp.float32))
    top_vals, top_idx = lax.top_k(logits, TOP_K)
    gates = jax.nn.softmax(top_vals, axis=-1)
    combine = jnp.sum(gates[..., None] * jax.nn.one_hot(top_idx, N_EXPERTS, dtype=jnp.float32), axis=-2)
    y = jnp.zeros(x.shape, jnp.float32)
    for e in range(N_EXPERTS):
        y = y + combine[..., e:e + 1] * _swiglu(x, w_gate[e], w_up[e], w_down[e])
    return y.astype(x.dtype)


def _layer(x, ka_prev, va_prev, kb_prev, vb_prev, pos0, keep_b, w_in, sink, w_br_a, w_br_b, w_out,
           ln1_g, ln1_b, ln2_g, ln2_b, channel):
    y, ka, va, kb, vb = _token_mixer(x, ka_prev, va_prev, kb_prev, vb_prev, pos0, keep_b,
                                     w_in, sink, w_br_a, w_br_b, w_out)
    x = _layer_norm(ALPHA * x + y, ln1_g, ln1_b)
    x = _layer_norm(ALPHA * x + channel(x), ln2_g, ln2_b)
    return x, ka, va, kb, vb


def setup_inputs(seed: int = 0) -> dict:
    key = jax.random.key(seed)
    ks = jax.random.split(key, 24)
    l_b = min(W_MAX, PAST_LEN)

    def nrm(k, shape, scale):
        return jax.random.normal(k, shape, jnp.float32) * scale

    return {
        'x_prompt': nrm(ks[0], (BATCH, SEQ, D_MODEL), 1.0),
        'x_sample': nrm(ks[1], (DEC_BATCH, DEC_SEQ, D_MODEL), 1.0),
        'cache_a_k': nrm(ks[2], (DEPTH, DEC_BATCH, WIN_A, KV_A, HEAD_DIM), 1.0),
        'cache_a_v': nrm(ks[3], (DEPTH, DEC_BATCH, WIN_A, KV_A, HEAD_DIM), 1.0),
        'cache_b_k': nrm(ks[4], (DEPTH, DEC_BATCH, l_b, H_B, HEAD_DIM), 1.0),
        'cache_b_v': nrm(ks[5], (DEPTH, DEC_BATCH, l_b, H_B, HEAD_DIM), 1.0),
        'w_in': nrm(ks[6], (DEPTH, D_MODEL, IN_COLS), D_MODEL ** -0.5),
        'attn_sink': nrm(ks[7], (DEPTH, H_A), 1.0),
        'w_branch_a': nrm(ks[8], (DEPTH, H_A * HEAD_DIM, D_MODEL), (H_A * HEAD_DIM) ** -0.5),
        'w_branch_b': nrm(ks[9], (DEPTH, H_B * HEAD_DIM, D_MODEL), (H_B * HEAD_DIM) ** -0.5),
        'w_out': nrm(ks[10], (DEPTH, D_MODEL, D_MODEL), BETA * D_MODEL ** -0.5),
        'ln1_g': 1.0 + nrm(ks[11], (DEPTH, D_MODEL), 0.02),
        'ln1_b': nrm(ks[12], (DEPTH, D_MODEL), 0.02),
        'ln2_g': 1.0 + nrm(ks[13], (DEPTH, D_MODEL), 0.02),
        'ln2_b': nrm(ks[14], (DEPTH, D_MODEL), 0.02),
        'ffn_w_gate': nrm(ks[15], (N_DENSE, D_MODEL, D_FF), D_MODEL ** -0.5),
        'ffn_w_up': nrm(ks[16], (N_DENSE, D_MODEL, D_FF), D_MODEL ** -0.5),
        'ffn_w_down': nrm(ks[17], (N_DENSE, D_FF, D_MODEL), BETA * D_FF ** -0.5),
        'moe_router': nrm(ks[18], (N_MOE, D_MODEL, N_EXPERTS), D_MODEL ** -0.5),
        'moe_w_gate': nrm(ks[19], (N_MOE, N_EXPERTS, D_MODEL, D_EXPERT), D_MODEL ** -0.5),
        'moe_w_up': nrm(ks[20], (N_MOE, N_EXPERTS, D_MODEL, D_EXPERT), D_MODEL ** -0.5),
        'moe_w_down': nrm(ks[21], (N_MOE, N_EXPERTS, D_EXPERT, D_MODEL), BETA * D_EXPERT ** -0.5),
    }


def reference(x_prompt, x_sample, cache_a_k, cache_a_v, cache_b_k, cache_b_v,
              w_in, attn_sink, w_branch_a, w_branch_b, w_out, ln1_g, ln1_b, ln2_g, ln2_b,
              ffn_w_gate, ffn_w_up, ffn_w_down, moe_router, moe_w_gate, moe_w_up, moe_w_down):
    bp, tp_len = x_prompt.shape[0], x_prompt.shape[1]
    keep_b_prompt = min(W_MAX, tp_len)
    keep_b_sample = cache_b_k.shape[2]
    zeros_a = jnp.zeros((bp, WIN_A, KV_A, HEAD_DIM), x_prompt.dtype)
    zeros_b = jnp.zeros((bp, W_MAX, H_B, HEAD_DIM), x_prompt.dtype)
    pad_b = ((0, 0), (W_MAX - keep_b_sample, 0), (0, 0), (0, 0))
    xp, xs = x_prompt, x_sample
    pak, pav, pbk, pbv, sak, sav, sbk, sbv = [], [], [], [], [], [], [], []
    for l in range(DEPTH):
        i = l // 2
        if l % 2 == 0:
            def channel(h, i=i):
                return _swiglu(h, ffn_w_gate[i], ffn_w_up[i], ffn_w_down[i])
        else:
            def channel(h, i=i):
                return _moe_swiglu(h, moe_router[i], moe_w_gate[i], moe_w_up[i], moe_w_down[i])
        lw = (w_in[l], attn_sink[l], w_branch_a[l], w_branch_b[l], w_out[l],
              ln1_g[l], ln1_b[l], ln2_g[l], ln2_b[l])
        xp, ka, va, kb, vb = _layer(xp, zeros_a, zeros_a, zeros_b, zeros_b, 0, keep_b_prompt, *lw, channel)
        pak.append(ka); pav.append(va); pbk.append(kb); pbv.append(vb)
        xs, ka, va, kb, vb = _layer(xs, cache_a_k[l], cache_a_v[l],
                                    jnp.pad(cache_b_k[l], pad_b), jnp.pad(cache_b_v[l], pad_b),
                                    PAST_LEN, keep_b_sample, *lw, channel)
        sak.append(ka); sav.append(va); sbk.append(kb); sbv.append(vb)
    return (xp, xs,
            jnp.stack(pak), jnp.stack(pav), jnp.stack(pbk), jnp.stack(pbv),
            jnp.stack(sak), jnp.stack(sav), jnp.stack(sbk), jnp.stack(sbv))
```

```python
import functools
import math

import jax
import jax.numpy as jnp
import numpy as np
from jax import lax
from jax.experimental import pallas as pl
from jax.experimental.pallas import tpu as pltpu

F32 = jnp.float32
BF16 = jnp.bfloat16

D_MODEL = 1024
HEAD_DIM = 64
H_A = 8
KV_A = 2
WIN_A = 128
H_B = 8
DILATED_PAIRS = ((128, 1), (512, 4), (2048, 16))
W_MAX = 2048
N_EXPERTS = 8
LN_EPS = 1e-5
NEG_INF = -1e30
QA_W = H_A * HEAD_DIM
KVA_W = 2 * KV_A * HEAD_DIM
HB_W = H_B * HEAD_DIM
IN_COLS = QA_W + KVA_W + 3 * HB_W + 2 * D_MODEL
QK_BLOCK = 128
LANES = 128
VMEM_LIMIT_BYTES = 56 * 1024 * 1024


def _params(*sem):
    return pltpu.CompilerParams(dimension_semantics=sem, vmem_limit_bytes=VMEM_LIMIT_BYTES)


def _row_tile(m, cap):
    t = cap
    while m % t:
        t //= 2
    assert t >= 8, (m, cap)
    return t


def _sigmoid(x):
    return 1.0 / (1.0 + jnp.exp(-x))


def _layer_norm(z, g, b):
    mu = jnp.mean(z, axis=-1, keepdims=True)
    zc = z - mu
    var = jnp.mean(zc * zc, axis=-1, keepdims=True)
    return zc * lax.rsqrt(var + LN_EPS) * g + b


def _dot(a, b):
    return jnp.dot(a, b, preferred_element_type=F32)


def _dot_nt(a, b):
    return lax.dot_general(a, b, (((1,), (1,)), ((), ())), preferred_element_type=F32)


def _inproj_kernel(x_ref, w_ref, qa_ref, kva_ref, qb_ref, kb_ref, vb_ref, g_ref):
    x = x_ref[...]
    scale = HEAD_DIM ** -0.5

    def mm(c0, c1):
        return _dot(x, w_ref[:, c0:c1])

    c = 0
    qa_ref[...] = mm(c, c + QA_W) * scale
    c += QA_W
    kva_ref[...] = mm(c, c + KVA_W)
    c += KVA_W
    qb_ref[...] = mm(c, c + HB_W) * scale
    c += HB_W
    kb_ref[...] = mm(c, c + HB_W)
    c += HB_W
    vb_ref[...] = mm(c, c + HB_W)
    c += HB_W
    g_ref[...] = mm(c, c + 2 * D_MODEL)


def _inproj(xb, w):
    m = xb.shape[0]
    tm = _row_tile(m, 512)
    widths = (QA_W, KVA_W, HB_W, HB_W, HB_W, 2 * D_MODEL)
    return pl.pallas_call(
        _inproj_kernel,
        grid=(m // tm,),
        in_specs=[pl.BlockSpec((tm, D_MODEL), lambda i: (i, 0)),
                  pl.BlockSpec((D_MODEL, IN_COLS), lambda i: (0, 0))],
        out_specs=[pl.BlockSpec((tm, wd), lambda i: (i, 0)) for wd in widths],
        out_shape=[jax.ShapeDtypeStruct((m, wd), F32) for wd in widths],
        compiler_params=_params("arbitrary"),
        name="inproj",
    )(xb, w)


def _pair_count(dist):
    c = jnp.zeros(dist.shape, jnp.int32)
    for w, d in DILATED_PAIRS:
        hit = jnp.where((dist & (d - 1)) == 0, 1, 0)
        c = c + jnp.where(dist <= w, hit, 0)
    return jnp.where(dist >= 0, c, 0)


def _log_count(c):
    return jnp.where(c == 3, math.log(3.0), jnp.where(c == 2, math.log(2.0), 0.0))


def _attn_prompt_kernel(slopes_ref, sink_ref, qa_ref, qb_ref, kva_ref, kb_ref, vb_ref,
                        oa_ref, ob_ref, ka_s, va_s, kb_s, vb_s, *, seq):
    i = pl.program_id(1)
    blk = QK_BLOCK

    @pl.when(i == 0)
    def _stage_keys():
        def body(c, carry):
            r0 = pl.multiple_of(c * blk, blk)
            kva = kva_ref[pl.ds(r0, blk), :]
            kb = kb_ref[pl.ds(r0, blk), :]
            vb = vb_ref[pl.ds(r0, blk), :]
            for g in range(KV_A):
                ka_s[g, pl.ds(r0, blk), :] = kva[:, g * HEAD_DIM:(g + 1) * HEAD_DIM].astype(BF16)
                va_s[g, pl.ds(r0, blk), :] = kva[:, (KV_A + g) * HEAD_DIM:(KV_A + g + 1) * HEAD_DIM].astype(BF16)
            for h in range(H_B):
                kb_s[h, pl.ds(r0, blk), :] = kb[:, h * HEAD_DIM:(h + 1) * HEAD_DIM].astype(BF16)
                vb_s[h, pl.ds(r0, blk), :] = vb[:, h * HEAD_DIM:(h + 1) * HEAD_DIM].astype(BF16)
            return carry
        lax.fori_loop(0, seq // blk, body, 0)

    row = lax.broadcasted_iota(jnp.int32, (blk, blk), 0)
    col = lax.broadcasted_iota(jnp.int32, (blk, blk), 1)
    rc = row - col

    def run(q, k_s, v_s, hk, j_lo, slope, dilated):
        def body(j, carry):
            m, l, acc = carry
            r0 = pl.multiple_of(j * blk, blk)
            k = k_s[hk, pl.ds(r0, blk), :]
            v = v_s[hk, pl.ds(r0, blk), :]
            s = _dot_nt(q, k)
            dist = (i - j) * blk + rc
            s = s - slope * dist.astype(F32)
            if dilated:
                c = _pair_count(dist)
                s = jnp.where(c > 0, s + _log_count(c), NEG_INF)
            else:
                s = jnp.where(dist >= 0, jnp.where(dist <= WIN_A - 1, s, NEG_INF), NEG_INF)
            m_new = jnp.maximum(m, jnp.max(s, axis=-1, keepdims=True))
            alpha = jnp.exp(m - m_new)
            p = jnp.exp(s - m_new)
            l = alpha * l + jnp.sum(p, axis=-1, keepdims=True)
            acc = alpha * acc + _dot(p.astype(BF16), v)
            return m_new, l, acc
        init = (jnp.full((blk, 1), NEG_INF, F32), jnp.zeros((blk, 1), F32),
                jnp.zeros((blk, HEAD_DIM), F32))
        return lax.fori_loop(j_lo, i + 1, body, init)

    qa = qa_ref[...]
    qb = qb_ref[...]
    for h in range(H_A):
        q = qa[:, h * HEAD_DIM:(h + 1) * HEAD_DIM].astype(BF16)
        m, l, acc = run(q, ka_s, va_s, h // (H_A // KV_A), jnp.maximum(i - 1, 0),
                        slopes_ref[2 * h], False)
        sk = sink_ref[h]
        m2 = jnp.maximum(m, sk)
        a = jnp.exp(m - m2)
        l2 = l * a + jnp.exp(sk - m2)
        oa_ref[:, h * HEAD_DIM:(h + 1) * HEAD_DIM] = (acc * a / l2).astype(BF16)
    for h in range(H_B):
        q = qb[:, h * HEAD_DIM:(h + 1) * HEAD_DIM].astype(BF16)
        m, l, acc = run(q, kb_s, vb_s, h, 0, slopes_ref[2 * h + 1], True)
        ob_ref[:, h * HEAD_DIM:(h + 1) * HEAD_DIM] = (acc / l).astype(BF16)


def _attn_prompt(slopes, sink, qa, qb, kva, kb, vb, batch, seq):
    nblk = seq // QK_BLOCK
    mp = batch * seq
    smem = pl.BlockSpec(memory_space=pltpu.SMEM)
    qspec = pl.BlockSpec((QK_BLOCK, QA_W), lambda b, i: (b * nblk + i, 0))
    return pl.pallas_call(
        functools.partial(_attn_prompt_kernel, seq=seq),
        grid=(batch, nblk),
        in_specs=[smem, smem, qspec, qspec,
                  pl.BlockSpec((seq, KVA_W), lambda b, i: (b, 0)),
                  pl.BlockSpec((seq, HB_W), lambda b, i: (b, 0)),
                  pl.BlockSpec((seq, HB_W), lambda b, i: (b, 0))],
        out_specs=[qspec, qspec],
        out_shape=[jax.ShapeDtypeStruct((mp, QA_W), BF16), jax.ShapeDtypeStruct((mp, HB_W), BF16)],
        scratch_shapes=[pltpu.VMEM((KV_A, seq, HEAD_DIM), BF16), pltpu.VMEM((KV_A, seq, HEAD_DIM), BF16),
                        pltpu.VMEM((H_B, seq, HEAD_DIM), BF16), pltpu.VMEM((H_B, seq, HEAD_DIM), BF16)],
        compiler_params=_params("arbitrary", "arbitrary"),
        name="attn_prompt",
    )(slopes, sink, qa, qb, kva, kb, vb)


SAMPLE_PAD = 16


def _attn_sample_kernel(sink_ref, qa_ref, qb_ref, kva_ref, kb_ref, vb_ref,
                        cak_ref, cav_ref, cbk_ref, cbv_ref, bias_a_ref, bias_b_ref,
                        oa_ref, ob_ref, ka_s, va_s, kb_s, vb_s, *, nq, la, lb):
    rows = H_A * nq

    @pl.when(pl.program_id(0) == 0)
    def _zero_pad_rows():
        ka_s[...] = jnp.zeros(ka_s.shape, BF16)
        va_s[...] = jnp.zeros(va_s.shape, BF16)
        kb_s[...] = jnp.zeros(kb_s.shape, BF16)
        vb_s[...] = jnp.zeros(vb_s.shape, BF16)

    def widen(x):
        rep = H_A // KV_A
        return jnp.concatenate([x[:, g * HEAD_DIM:(g + 1) * HEAD_DIM] for g in range(KV_A) for _ in range(rep)],
                               axis=1)

    def pad_new(x):
        return jnp.concatenate([x, jnp.zeros((SAMPLE_PAD - nq, x.shape[1]), F32)], axis=0)

    kva = kva_ref[...]
    ka_s[0:la, :] = widen(cak_ref[...]).astype(BF16)
    va_s[0:la, :] = widen(cav_ref[...]).astype(BF16)
    ka_s[la:la + SAMPLE_PAD, :] = pad_new(widen(kva[:, 0:KV_A * HEAD_DIM])).astype(BF16)
    va_s[la:la + SAMPLE_PAD, :] = pad_new(widen(kva[:, KV_A * HEAD_DIM:])).astype(BF16)
    chunk = 256
    for c in range(lb // chunk):
        kb_s[c * chunk:(c + 1) * chunk, :] = cbk_ref[c * chunk:(c + 1) * chunk, :].astype(BF16)
        vb_s[c * chunk:(c + 1) * chunk, :] = cbv_ref[c * chunk:(c + 1) * chunk, :].astype(BF16)
    kb_s[lb:lb + SAMPLE_PAD, :] = pad_new(kb_ref[...]).astype(BF16)
    vb_s[lb:lb + SAMPLE_PAD, :] = pad_new(vb_ref[...]).astype(BF16)

    r = lax.broadcasted_iota(jnp.int32, (rows, QA_W), 0)
    c = lax.broadcasted_iota(jnp.int32, (rows, QA_W), 1)
    own = (r // nq) == (c // HEAD_DIM)

    def attend(q, k_s, v_s, bias_ref, sink):
        qbd = jnp.where(own, jnp.concatenate([q] * H_A, axis=0), 0.0).astype(BF16)
        s = _dot_nt(qbd, k_s[...]) + bias_ref[...]
        m = jnp.max(s, axis=-1, keepdims=True)
        if sink is not None:
            m = jnp.maximum(m, sink)
        p = jnp.exp(s - m)
        l = jnp.sum(p, axis=-1, keepdims=True)
        if sink is not None:
            l = l + jnp.exp(sink - m)
        o = jnp.where(own, _dot(p.astype(BF16), v_s[...]), 0.0) / l
        out = o[0:nq]
        for h in range(1, H_A):
            out = out + o[h * nq:(h + 1) * nq]
        return out

    hrow = lax.broadcasted_iota(jnp.int32, (rows, 1), 0) // nq
    sink = jnp.zeros((rows, 1), F32)
    for h in range(H_A):
        sink = jnp.where(hrow == h, sink_ref[h], sink)
    oa_ref[...] = attend(qa_ref[...], ka_s, va_s, bias_a_ref, sink)
    ob_ref[...] = attend(qb_ref[...], kb_s, vb_s, bias_b_ref, None)


def _sample_bias(slopes, nq, cache_len, max_dist, dilated, padded_len):
    i = jnp.arange(nq, dtype=jnp.int32)[:, None]
    key = jnp.arange(padded_len, dtype=jnp.int32)[None, :]
    dist = cache_len + i - key
    real = key < cache_len + nq
    if dilated:
        cnt = _pair_count(dist)
        ok = real & (cnt > 0)
        extra = _log_count(cnt).astype(F32)
    else:
        ok = real & (dist >= 0) & (dist <= max_dist)
        extra = jnp.zeros(dist.shape, F32)
    bias = -slopes[:, None, None] * dist.astype(F32)[None] + extra[None]
    bias = jnp.where(ok[None], bias, NEG_INF)
    return bias.reshape(slopes.shape[0] * nq, padded_len).astype(F32)


def _attn_sample(layer, sink, qa, qb, kva, kb, vb, cak, cav, cbk, cbv, bias_a, bias_b, row0, nb, nq):
    la, lb = cak.shape[2], cbk.shape[2]
    pa, pb = bias_a.shape[1], bias_b.shape[1]
    blk0 = row0 // nq
    smem = pl.BlockSpec(memory_space=pltpu.SMEM)

    def rows(wd):
        return pl.BlockSpec((nq, wd), lambda b: (blk0 + b, 0))

    def cache(length, wd):
        return pl.BlockSpec((None, None, length, wd), lambda b: (layer, b, 0, 0))

    def const(shape):
        return pl.BlockSpec(shape, lambda b: (0, 0))

    out_spec = pl.BlockSpec((nq, QA_W), lambda b: (b, 0))
    return pl.pallas_call(
        functools.partial(_attn_sample_kernel, nq=nq, la=la, lb=lb),
        grid=(nb,),
        in_specs=[smem, rows(QA_W), rows(HB_W), rows(KVA_W), rows(HB_W), rows(HB_W),
                  cache(la, KV_A * HEAD_DIM), cache(la, KV_A * HEAD_DIM), cache(lb, HB_W), cache(lb, HB_W),
                  const(bias_a.shape), const(bias_b.shape)],
        out_specs=[out_spec, out_spec],
        out_shape=[jax.ShapeDtypeStruct((nb * nq, QA_W), F32), jax.ShapeDtypeStruct((nb * nq, HB_W), F32)],
        scratch_shapes=[pltpu.VMEM((pa, QA_W), BF16), pltpu.VMEM((pa, QA_W), BF16),
                        pltpu.VMEM((pb, HB_W), BF16), pltpu.VMEM((pb, HB_W), BF16)],
        compiler_params=_params("arbitrary"),
        name="attn_sample",
    )(sink, qa, qb, kva, kb, vb, cak, cav, cbk, cbv, bias_a, bias_b)


def _post_kernel(*refs, alpha, with_router):
    if with_router:
        (x_ref, oa_ref, ob_ref, g_ref, wa_ref, wb_ref, wo_ref, lg_ref, lb_ref, wr_ref,
         x1_ref, x1b_ref, comb_ref) = refs
    else:
        (x_ref, oa_ref, ob_ref, g_ref, wa_ref, wb_ref, wo_ref, lg_ref, lb_ref,
         x1_ref, x1b_ref) = refs
    a = _dot(oa_ref[...], wa_ref[...])
    b = _dot(ob_ref[...], wb_ref[...])
    g = g_ref[...]
    merged = _sigmoid(g[:, :D_MODEL]) * a + _sigmoid(g[:, D_MODEL:]) * b
    y = _dot(merged.astype(BF16), wo_ref[...])
    x1 = _layer_norm(alpha * x_ref[...] + y, lg_ref[...], lb_ref[...])
    x1_ref[...] = x1
    x1b_ref[...] = x1.astype(BF16)
    if with_router:
        logits = jnp.dot(x1, wr_ref[...], preferred_element_type=F32, precision=lax.Precision.HIGHEST)
        lane = lax.broadcasted_iota(jnp.int32, logits.shape, 1)
        logits = jnp.where(lane < N_EXPERTS, logits, -jnp.inf)
        t1 = jnp.max(logits, axis=-1, keepdims=True)
        lane = lane.astype(F32)
        i1 = jnp.min(jnp.where(logits == t1, lane, float(LANES)), axis=-1, keepdims=True)
        rest = jnp.where(lane == i1, -jnp.inf, logits)
        t2 = jnp.max(rest, axis=-1, keepdims=True)
        i2 = jnp.min(jnp.where(rest == t2, lane, float(LANES)), axis=-1, keepdims=True)
        e2 = jnp.exp(t2 - t1)
        den = 1.0 + e2
        comb_ref[...] = jnp.where(lane == i1, 1.0 / den, jnp.where(lane == i2, e2 / den, 0.0))


def _post(x, oa, ob, g, wa, wb, wo, ln_g, ln_b, alpha, w_router=None):
    m = x.shape[0]
    tm = _row_tile(m, 512)
    with_router = w_router is not None

    def rows(wd):
        return pl.BlockSpec((tm, wd), lambda i: (i, 0))

    def const(shape):
        return pl.BlockSpec(shape, lambda i: (0, 0))

    in_specs = [rows(D_MODEL), rows(QA_W), rows(HB_W), rows(2 * D_MODEL),
                const(wa.shape), const(wb.shape), const(wo.shape), const((1, D_MODEL)), const((1, D_MODEL))]
    args = [x, oa, ob, g, wa, wb, wo, ln_g, ln_b]
    out_specs = [rows(D_MODEL), rows(D_MODEL)]
    out_shape = [jax.ShapeDtypeStruct((m, D_MODEL), F32), jax.ShapeDtypeStruct((m, D_MODEL), BF16)]
    if with_router:
        in_specs.append(const(w_router.shape))
        args.append(w_router)
        out_specs.append(rows(LANES))
        out_shape.append(jax.ShapeDtypeStruct((m, LANES), F32))
    return pl.pallas_call(
        functools.partial(_post_kernel, alpha=alpha, with_router=with_router),
        grid=(m // tm,),
        in_specs=in_specs, out_specs=out_specs, out_shape=out_shape,
        compiler_params=_params("arbitrary"),
        name="post_router" if with_router else "post",
    )(*args)


def _swiglu_chunk(xb, wg, wu, wd):
    hg = _dot(xb, wg)
    hu = _dot(xb, wu)
    return _dot((hg * _sigmoid(hg) * hu).astype(BF16), wd)


def _ffn_kernel(x_ref, xb_ref, wg_ref, wu_ref, wd_ref, lg_ref, lb_ref, x2_ref, x2b_ref, acc_ref,
                *, alpha, nf):
    f = pl.program_id(1)

    @pl.when(f == 0)
    def _():
        acc_ref[...] = jnp.zeros(acc_ref.shape, F32)

    acc_ref[...] += _swiglu_chunk(xb_ref[...], wg_ref[...], wu_ref[...], wd_ref[...])

    @pl.when(f == nf - 1)
    def _():
        x2 = _layer_norm(alpha * x_ref[...] + acc_ref[...], lg_ref[...], lb_ref[...])
        x2_ref[...] = x2
        x2b_ref[...] = x2.astype(BF16)


def _ffn(x, xb, wg, wu, wd, ln_g, ln_b, alpha):
    m = x.shape[0]
    tm = _row_tile(m, 512)
    d_ff = wg.shape[1]
    tf = 256
    nf = d_ff // tf
    rows = pl.BlockSpec((tm, D_MODEL), lambda i, f: (i, 0))
    vec = pl.BlockSpec((1, D_MODEL), lambda i, f: (0, 0))
    return pl.pallas_call(
        functools.partial(_ffn_kernel, alpha=alpha, nf=nf),
        grid=(m // tm, nf),
        in_specs=[rows, rows,
                  pl.BlockSpec((D_MODEL, tf), lambda i, f: (0, f)),
                  pl.BlockSpec((D_MODEL, tf), lambda i, f: (0, f)),
                  pl.BlockSpec((tf, D_MODEL), lambda i, f: (f, 0)),
                  vec, vec],
        out_specs=[rows, rows],
        out_shape=[jax.ShapeDtypeStruct((m, D_MODEL), F32), jax.ShapeDtypeStruct((m, D_MODEL), BF16)],
        scratch_shapes=[pltpu.VMEM((tm, D_MODEL), F32)],
        compiler_params=_params("arbitrary", "arbitrary"),
        name="ffn",
    )(x, xb, wg, wu, wd, ln_g, ln_b)


def _moe_kernel(x_ref, xb_ref, comb_ref, wg_ref, wu_ref, wd_ref, lg_ref, lb_ref, x2_ref, x2b_ref, acc_ref,
                *, alpha, nf):
    e = pl.program_id(1)
    f = pl.program_id(2)

    @pl.when((e == 0) & (f == 0))
    def _():
        acc_ref[...] = jnp.zeros(acc_ref.shape, F32)

    comb = comb_ref[...]
    lane = lax.broadcasted_iota(jnp.int32, comb.shape, 1)
    gate = jnp.sum(jnp.where(lane == e, comb, 0.0), axis=-1, keepdims=True)
    acc_ref[...] += gate * _swiglu_chunk(xb_ref[...], wg_ref[...], wu_ref[...], wd_ref[...])

    @pl.when((e == N_EXPERTS - 1) & (f == nf - 1))
    def _():
        x2 = _layer_norm(alpha * x_ref[...] + acc_ref[...], lg_ref[...], lb_ref[...])
        x2_ref[...] = x2
        x2b_ref[...] = x2.astype(BF16)


def _moe(layer_moe, x, xb, comb, wg, wu, wd, ln_g, ln_b, alpha):
    m = x.shape[0]
    tm = _row_tile(m, 512)
    d_e = wg.shape[3]
    tf = 512
    nf = d_e // tf
    rows = pl.BlockSpec((tm, D_MODEL), lambda i, e, f: (i, 0))
    vec = pl.BlockSpec((1, D_MODEL), lambda i, e, f: (0, 0))
    return pl.pallas_call(
        functools.partial(_moe_kernel, alpha=alpha, nf=nf),
        grid=(m // tm, N_EXPERTS, nf),
        in_specs=[rows, rows, pl.BlockSpec((tm, LANES), lambda i, e, f: (i, 0)),
                  pl.BlockSpec((None, None, D_MODEL, tf), lambda i, e, f: (layer_moe, e, 0, f)),
                  pl.BlockSpec((None, None, D_MODEL, tf), lambda i, e, f: (layer_moe, e, 0, f)),
                  pl.BlockSpec((None, None, tf, D_MODEL), lambda i, e, f: (layer_moe, e, f, 0)),
                  vec, vec],
        out_specs=[rows, rows],
        out_shape=[jax.ShapeDtypeStruct((m, D_MODEL), F32), jax.ShapeDtypeStruct((m, D_MODEL), BF16)],
        scratch_shapes=[pltpu.VMEM((tm, D_MODEL), F32)],
        compiler_params=_params("arbitrary", "arbitrary", "arbitrary"),
        name="moe",
    )(x, xb, comb, wg, wu, wd, ln_g, ln_b)


def kernel(x_prompt, x_sample, cache_a_k, cache_a_v, cache_b_k, cache_b_v, w_in, attn_sink, w_branch_a,
           w_branch_b, w_out, ln1_g, ln1_b, ln2_g, ln2_b, ffn_w_gate, ffn_w_up, ffn_w_down, moe_router,
           moe_w_gate, moe_w_up, moe_w_down):
    depth = w_in.shape[0]
    bp, seq, _ = x_prompt.shape
    nb, nq, _ = x_sample.shape
    la, lb = cache_a_k.shape[2], cache_b_k.shape[2]
    assert seq % QK_BLOCK == 0 and seq <= W_MAX and la == WIN_A and lb == W_MAX and nq <= 8
    mp, ms = bp * seq, nb * nq
    alpha = (2.0 * depth) ** 0.25

    n_heads = H_A + H_B
    slopes = jnp.exp2(-8.0 * jnp.arange(1, n_heads + 1, dtype=F32) / n_heads)
    pa = 2 * LANES
    pb = lb + LANES
    bias_a = _sample_bias(slopes[0::2], nq, la, WIN_A - 1, False, pa)
    bias_b = _sample_bias(slopes[1::2], nq, lb, W_MAX, True, pb)

    cak = cache_a_k.reshape(depth, nb, la, KV_A * HEAD_DIM)
    cav = cache_a_v.reshape(depth, nb, la, KV_A * HEAD_DIM)
    cbk = cache_b_k.reshape(depth, nb, lb, HB_W)
    cbv = cache_b_v.reshape(depth, nb, lb, HB_W)

    w_in_b = w_in.astype(BF16)
    wa_b, wb_b, wo_b = w_branch_a.astype(BF16), w_branch_b.astype(BF16), w_out.astype(BF16)
    fg_b, fu_b, fd_b = ffn_w_gate.astype(BF16), ffn_w_up.astype(BF16), ffn_w_down.astype(BF16)
    mg_b, mu_b, md_b = moe_w_gate.astype(BF16), moe_w_up.astype(BF16), moe_w_down.astype(BF16)
    router_pad = jnp.pad(moe_router.astype(F32), ((0, 0), (0, 0), (0, LANES - N_EXPERTS)))

    x = jnp.concatenate([x_prompt.reshape(mp, D_MODEL), x_sample.reshape(ms, D_MODEL)], axis=0)
    xb = x.astype(BF16)

    pak, pav, pbk, pbv, sak, sav, sbk, sbv = [], [], [], [], [], [], [], []
    for l in range(depth):
        qa, kva, qb, kb, vb, g = _inproj(xb, w_in_b[l])
        oa_p, ob_p = _attn_prompt(slopes, attn_sink[l], qa, qb, kva, kb, vb, bp, seq)
        oa_s, ob_s = _attn_sample(l, attn_sink[l], qa, qb, kva, kb, vb, cak, cav, cbk, cbv,
                                  bias_a, bias_b, mp, nb, nq)
        oa = jnp.concatenate([oa_p, oa_s.astype(BF16)], axis=0)
        ob = jnp.concatenate([ob_p, ob_s.astype(BF16)], axis=0)

        kva_p = kva[:mp].reshape(bp, seq, 2, KV_A, HEAD_DIM)[:, seq - WIN_A:]
        pak.append(kva_p[:, :, 0])
        pav.append(kva_p[:, :, 1])
        pbk.append(kb[:mp].reshape(bp, seq, H_B, HEAD_DIM))
        pbv.append(vb[:mp].reshape(bp, seq, H_B, HEAD_DIM))
        kva_s = kva[mp:].reshape(nb, nq, 2, KV_A, HEAD_DIM)
        sak.append(kva_s[:, :, 0])
        sav.append(kva_s[:, :, 1])
        sbk.append(kb[mp:].reshape(nb, nq, H_B, HEAD_DIM))
        sbv.append(vb[mp:].reshape(nb, nq, H_B, HEAD_DIM))

        ln1 = (ln1_g[l].reshape(1, D_MODEL), ln1_b[l].reshape(1, D_MODEL))
        ln2 = (ln2_g[l].reshape(1, D_MODEL), ln2_b[l].reshape(1, D_MODEL))
        i = l // 2
        if l % 2 == 0:
            x1, x1b = _post(x, oa, ob, g, wa_b[l], wb_b[l], wo_b[l], *ln1, alpha)
            x, xb = _ffn(x1, x1b, fg_b[i], fu_b[i], fd_b[i], *ln2, alpha)
        else:
            x1, x1b, comb = _post(x, oa, ob, g, wa_b[l], wb_b[l], wo_b[l], *ln1, alpha, router_pad[i])
            x, xb = _moe(i, x1, x1b, comb, mg_b, mu_b, md_b, *ln2, alpha)

    def shifted(cache, new):
        return jnp.concatenate([cache[:, :, nq:], jnp.stack(new)], axis=2)

    return (x[:mp].reshape(bp, seq, D_MODEL), x[mp:].reshape(nb, nq, D_MODEL),
            jnp.stack(pak), jnp.stack(pav), jnp.stack(pbk), jnp.stack(pbv),
            shifted(cache_a_k, sak), shifted(cache_a_v, sav),
            shifted(cache_b_k, sbk), shifted(cache_b_v, sbv))
```

```python
import functools
import math

import jax
import jax.numpy as jnp
from jax import lax
from jax.experimental import pallas as pl
from jax.experimental.pallas import tpu as pltpu

F32 = jnp.float32
BF16 = jnp.bfloat16

D_MODEL = 1024
HEAD_DIM = 64
H_A = 8
KV_A = 2
WIN_A = 128
H_B = 8
DILATED_PAIRS = ((128, 1), (512, 4), (2048, 16))
W_MAX = 2048
N_EXPERTS = 8
LN_EPS = 1e-5
NEG_INF = -1e30
QA_W = H_A * HEAD_DIM
KVA_W = 2 * KV_A * HEAD_DIM
HB_W = H_B * HEAD_DIM
IN_COLS = QA_W + KVA_W + 3 * HB_W + 2 * D_MODEL
LANES = 128
VMEM_LIMIT_BYTES = 56 * 1024 * 1024


def _params(*sem):
    return pltpu.CompilerParams(dimension_semantics=sem, vmem_limit_bytes=VMEM_LIMIT_BYTES)


def _row_tile(m, cap):
    t = cap
    while m % t:
        t //= 2
    assert t >= 8, (m, cap)
    return t


def _sigmoid(x):
    return 1.0 / (1.0 + jnp.exp(-x))


def _layer_norm(z, g, b):
    mu = jnp.mean(z, axis=-1, keepdims=True)
    zc = z - mu
    var = jnp.mean(zc * zc, axis=-1, keepdims=True)
    return zc * lax.rsqrt(var + LN_EPS) * g + b


def _dot(a, b):
    return jnp.dot(a, b, preferred_element_type=F32)


def _dot_nt(a, b):
    return lax.dot_general(a, b, (((1,), (1,)), ((), ())), preferred_element_type=F32)


KVT_ROWS = 2 * KV_A * HEAD_DIM + 2 * HB_W


def _inproj_kernel(x_ref, w_ref, wt_ref, qa_ref, kva_ref, qb_ref, kb_ref, vb_ref, g_ref,
                   kat_ref, vat_ref, kbt_ref, vbt_ref, *, prompt_tiles):
    x = x_ref[...]
    scale = HEAD_DIM ** -0.5

    def mm(c0, c1):
        return _dot(x, w_ref[:, c0:c1])

    c = 0
    qa_ref[...] = mm(c, c + QA_W) * scale
    c += QA_W
    kva_ref[...] = mm(c, c + KVA_W)
    c += KVA_W
    qb_ref[...] = mm(c, c + HB_W) * scale
    c += HB_W
    kb_ref[...] = mm(c, c + HB_W)
    c += HB_W
    vb_ref[...] = mm(c, c + HB_W)
    c += HB_W
    g_ref[...] = mm(c, c + 2 * D_MODEL)

    @pl.when(pl.program_id(0) < prompt_tiles)
    def _seq_minor_outputs():
        def mt(r0, r1):
            return _dot_nt(wt_ref[r0:r1, :], x)
        ga = KV_A * HEAD_DIM
        kat_ref[...] = mt(0, ga)
        vat_ref[...] = mt(ga, 2 * ga)
        kbt_ref[...] = mt(2 * ga, 2 * ga + HB_W)
        vbt_ref[...] = mt(2 * ga + HB_W, 2 * ga + 2 * HB_W)


def _inproj(xb, w, wt, bp, seq):
    m = xb.shape[0]
    tm = _row_tile(math.gcd(m, seq), 512)
    tiles_per_seq = seq // tm
    prompt_tiles = bp * tiles_per_seq
    widths = (QA_W, KVA_W, HB_W, HB_W, HB_W, 2 * D_MODEL)
    ga = KV_A * HEAD_DIM

    def seq_minor(rows):
        def index(i):
            j = jnp.minimum(i, prompt_tiles - 1)
            return (j // tiles_per_seq, 0, j % tiles_per_seq)
        return pl.BlockSpec((None, rows, tm), index)

    t_rows = (ga, ga, HB_W, HB_W)
    return pl.pallas_call(
        functools.partial(_inproj_kernel, prompt_tiles=prompt_tiles),
        grid=(m // tm,),
        in_specs=[pl.BlockSpec((tm, D_MODEL), lambda i: (i, 0)),
                  pl.BlockSpec((D_MODEL, IN_COLS), lambda i: (0, 0)),
                  pl.BlockSpec((KVT_ROWS, D_MODEL), lambda i: (0, 0))],
        out_specs=[pl.BlockSpec((tm, wd), lambda i: (i, 0)) for wd in widths] + [seq_minor(r) for r in t_rows],
        out_shape=[jax.ShapeDtypeStruct((m, wd), F32) for wd in widths]
        + [jax.ShapeDtypeStruct((bp, r, seq), F32) for r in t_rows],
        compiler_params=_params("arbitrary"),
        name="inproj",
    )(xb, w, wt)


BLK = 128
HEADS_PER_STEP = 2


def _tile_bias(slopes, n_q, n_k, shift, dist_scale, max_sub):
    r = jnp.arange(n_q, dtype=jnp.int32)[:, None]
    c = jnp.arange(n_k, dtype=jnp.int32)[None, :]
    dsub = shift + r - c
    ok = (dsub >= 0) & (dsub <= max_sub)
    b = -slopes[:, None, None] * (dist_scale * dsub).astype(F32)[None]
    full = jnp.where(ok[None], b, NEG_INF)
    first = jnp.where((ok & (c >= shift))[None], b, NEG_INF)
    return jnp.stack([full, first], axis=1).astype(F32)


def _softmax_tile(q, k, v, bias, sink=None):
    s = _dot_nt(q, k) + bias
    m = jnp.max(s, axis=-1, keepdims=True)
    if sink is not None:
        m = jnp.maximum(m, sink)
    p = jnp.exp(s - m)
    l = jnp.sum(p, axis=-1, keepdims=True)
    if sink is not None:
        l = l + jnp.exp(sink - m)
    o = _dot(p.astype(BF16), v) / l
    return o, m + jnp.log(l)


def _attn_prompt_kernel(sink_ref, qa_ref, qb_ref, kva_ref, kb_ref, vb_ref, ta_ref, t1_ref, t2_ref, t3_ref,
                        oa_ref, ob_ref,
                        qa_s, ka_s, va_s, q1_s, k1_s, v1_s, q4_s, k4_s, v4_s, q16_s, k16_s, v16_s, ro_s, rl_s,
                        *, seq):
    hp = pl.program_id(1)
    nblk = seq // BLK
    n4, n16 = seq // 4, seq // 16
    sub4 = n4 + BLK
    hd = HEAD_DIM

    @pl.when((pl.program_id(0) == 0) & (hp == 0))
    def _zero_key_pads():
        for ref in (ka_s, va_s, k1_s, v1_s, k4_s, v4_s):
            ref[...] = jnp.zeros(ref.shape, BF16)

    def heads(x):
        return [x[:, hh * hd:(hh + 1) * hd].astype(BF16) for hh in range(HEADS_PER_STEP)]

    group0 = (hp // (H_A // KV_A // HEADS_PER_STEP)) == 0

    def stage_natural(c, carry):
        r0 = pl.multiple_of(c * BLK, BLK)
        rows, prows = pl.ds(r0, BLK), pl.ds(r0 + BLK, BLK)
        kva = kva_ref[rows, :]
        ka_s[prows, :] = jnp.where(group0, kva[:, 0:hd], kva[:, hd:2 * hd]).astype(BF16)
        va_s[prows, :] = jnp.where(group0, kva[:, 2 * hd:3 * hd], kva[:, 3 * hd:4 * hd]).astype(BF16)
        for hh, (qa, qb, kb, vb) in enumerate(zip(heads(qa_ref[rows, :]), heads(qb_ref[rows, :]),
                                                  heads(kb_ref[rows, :]), heads(vb_ref[rows, :]))):
            qa_s[hh, rows, :] = qa
            q1_s[hh, rows, :] = qb
            k1_s[hh, prows, :] = kb
            v1_s[hh, prows, :] = vb
        return carry
    lax.fori_loop(0, nblk, stage_natural, 0, unroll=4)

    for r4 in range(4):
        def stage_mod4(c, carry, r4=r4):
            j0 = pl.multiple_of(c * BLK, BLK)
            src = pl.ds(r4 + 4 * j0, BLK, stride=4)
            for hh, (qb, kb, vb) in enumerate(zip(heads(qb_ref[src, :]), heads(kb_ref[src, :]),
                                                  heads(vb_ref[src, :]))):
                q4_s[hh, pl.ds(r4 * n4 + j0, BLK), :] = qb
                k4_s[hh, pl.ds(r4 * sub4 + BLK + j0, BLK), :] = kb
                v4_s[hh, pl.ds(r4 * sub4 + BLK + j0, BLK), :] = vb
            return carry
        lax.fori_loop(0, n4 // BLK, stage_mod4, 0, unroll=True)

    def stage_mod16(r, carry):
        src = pl.ds(r, n16, stride=16)
        dst = pl.ds(pl.multiple_of(r * n16, n16), n16)
        for hh, (qb, kb, vb) in enumerate(zip(heads(qb_ref[src, :]), heads(kb_ref[src, :]),
                                              heads(vb_ref[src, :]))):
            q16_s[hh, dst, :] = qb
            k16_s[hh, dst, :] = kb
            v16_s[hh, dst, :] = vb
        return carry
    lax.fori_loop(0, 16, stage_mod16, 0, unroll=4)

    def mixer_a(i, carry):
        r0 = pl.multiple_of(i * BLK, BLK)
        first = jnp.where(i == 0, 1, 0)
        k = ka_s[pl.ds(r0, 2 * BLK), :]
        v = va_s[pl.ds(r0, 2 * BLK), :]
        outs = []
        for hh in range(HEADS_PER_STEP):
            o, _ = _softmax_tile(qa_s[hh, pl.ds(r0, BLK), :], k, v, ta_ref[hh, first],
                                 sink_ref[HEADS_PER_STEP * hp + hh])
            outs.append(o)
        oa_ref[pl.ds(r0, BLK), :] = jnp.concatenate(outs, axis=1).astype(BF16)
        return carry
    lax.fori_loop(0, nblk, mixer_a, 0, unroll=True)

    def keep(pair, dst, o, lse):
        ro_s[pair, dst, :] = o
        rl_s[pair, dst, :] = jnp.broadcast_to(lse, o.shape)

    for hh in range(HEADS_PER_STEP):
        def pair_w128(i, carry, hh=hh):
            r0 = pl.multiple_of(i * BLK, BLK)
            o, lse = _softmax_tile(q1_s[hh, pl.ds(r0, BLK), :], k1_s[hh, pl.ds(r0, 2 * BLK), :],
                                   v1_s[hh, pl.ds(r0, 2 * BLK), :], t1_ref[hh, jnp.where(i == 0, 1, 0)])
            keep(0, pl.ds(r0, BLK), o, lse)
            return carry
        lax.fori_loop(0, nblk, pair_w128, 0, unroll=True)

        for r4 in range(4):
            def pair_w512(i, carry, hh=hh, r4=r4):
                j0 = pl.multiple_of(i * BLK, BLK)
                o, lse = _softmax_tile(q4_s[hh, pl.ds(r4 * n4 + j0, BLK), :],
                                       k4_s[hh, pl.ds(r4 * sub4 + j0, 2 * BLK), :],
                                       v4_s[hh, pl.ds(r4 * sub4 + j0, 2 * BLK), :],
                                       t2_ref[hh, jnp.where(i == 0, 1, 0)])
                keep(1, pl.ds(r4 + 4 * j0, BLK, stride=4), o, lse)
                return carry
            lax.fori_loop(0, n4 // BLK, pair_w512, 0, unroll=True)

        def pair_w2048(r, carry, hh=hh):
            src = pl.ds(pl.multiple_of(r * n16, n16), n16)
            o, lse = _softmax_tile(q16_s[hh, src, :], k16_s[hh, src, :], v16_s[hh, src, :], t3_ref[hh, 0])
            keep(2, pl.ds(r, n16, stride=16), o, lse)
            return carry
        lax.fori_loop(0, 16, pair_w2048, 0, unroll=True)

        def merge(i, carry, hh=hh):
            rows = pl.ds(pl.multiple_of(i * BLK, BLK), BLK)
            l0, l1, l2 = rl_s[0, rows, :], rl_s[1, rows, :], rl_s[2, rows, :]
            mx = jnp.maximum(l0, jnp.maximum(l1, l2))
            w0, w1, w2 = jnp.exp(l0 - mx), jnp.exp(l1 - mx), jnp.exp(l2 - mx)
            o = (w0 * ro_s[0, rows, :] + w1 * ro_s[1, rows, :] + w2 * ro_s[2, rows, :]) / (w0 + w1 + w2)
            ob_ref[rows, hh * hd:(hh + 1) * hd] = o.astype(BF16)
            return carry
        lax.fori_loop(0, nblk, merge, 0, unroll=4)


def _attn_prompt(sink, qa, qb, kva, kb, vb, tabs, batch, seq):
    mp = batch * seq
    n4 = seq // 4
    lanes = HEADS_PER_STEP * HEAD_DIM
    smem = pl.BlockSpec(memory_space=pltpu.SMEM)
    cols = pl.BlockSpec((seq, lanes), lambda b, hp: (b, hp))

    def tab(t):
        return pl.BlockSpec((HEADS_PER_STEP,) + t.shape[1:], lambda b, hp: (hp, 0, 0, 0))

    def per_head(rows):
        return pltpu.VMEM((HEADS_PER_STEP, rows, HEAD_DIM), BF16)

    return pl.pallas_call(
        functools.partial(_attn_prompt_kernel, seq=seq),
        grid=(batch, H_B // HEADS_PER_STEP),
        in_specs=[smem, cols, cols, pl.BlockSpec((seq, KVA_W), lambda b, hp: (b, 0)), cols, cols]
        + [tab(t) for t in tabs],
        out_specs=[cols, cols],
        out_shape=[jax.ShapeDtypeStruct((mp, QA_W), BF16), jax.ShapeDtypeStruct((mp, HB_W), BF16)],
        scratch_shapes=[per_head(seq), pltpu.VMEM((seq + BLK, HEAD_DIM), BF16), pltpu.VMEM((seq + BLK, HEAD_DIM), BF16),
                        per_head(seq), per_head(seq + BLK), per_head(seq + BLK),
                        per_head(seq), per_head(4 * (n4 + BLK)), per_head(4 * (n4 + BLK)),
                        per_head(seq), per_head(seq), per_head(seq),
                        pltpu.VMEM((3, seq, HEAD_DIM), F32), pltpu.VMEM((3, seq, HEAD_DIM), F32)],
        compiler_params=_params("arbitrary", "arbitrary"),
        name="attn_prompt",
    )(sink, qa, qb, kva, kb, vb, *tabs)


def _sample_bias(slopes, nq, cache_len, max_dist, dilated):
    i = jnp.arange(nq, dtype=jnp.int32)[:, None]

    def make(dist, real):
        if dilated:
            cnt = jnp.zeros(dist.shape, jnp.int32)
            for w, d in DILATED_PAIRS:
                cnt = cnt + jnp.where(((dist & (d - 1)) == 0) & (dist <= w), 1, 0)
            ok = real & (dist >= 0) & (cnt > 0)
            extra = jnp.where(cnt == 3, math.log(3.0), jnp.where(cnt == 2, math.log(2.0), 0.0)).astype(F32)
        else:
            ok = real & (dist >= 0) & (dist <= max_dist)
            extra = jnp.zeros(dist.shape, F32)
        bias = -slopes[:, None, None] * dist.astype(F32)[None] + extra[None]
        return jnp.where(ok[None], bias, NEG_INF).reshape(slopes.shape[0] * nq, dist.shape[1]).astype(F32)

    key = jnp.arange(cache_len, dtype=jnp.int32)[None, :]
    new = jnp.arange(LANES, dtype=jnp.int32)[None, :] - (LANES - nq)
    return make(cache_len + i - key, key >= 0), make(i - new, new >= 0)


def _attn_sample_kernel(*refs, nq, la, lb, chained):
    (sink_ref, qa_ref, qb_ref, kva_ref, kb_ref, vb_ref, cak_ref, cav_ref, cbk_ref, cbv_ref,
     bac_ref, ban_ref, bbc_ref, bbn_ref) = refs[:14]
    refs = refs[14 + (2 if chained else 0):]
    oa_ref, ob_ref, nak_ref, nav_ref, nbk_ref, nbv_ref, s_s, p_s = refs
    rows = H_A * nq
    hd = HEAD_DIM
    ga = KV_A * hd
    rep = H_A // KV_A

    def new_rows(x):
        return jnp.concatenate([jnp.zeros((LANES - nq, x.shape[1]), F32), x], axis=0)

    lane = lax.broadcasted_iota(jnp.int32, (1, LANES), 1)

    def shifted(cache_ref, out_ref, new_t, length, row_chunk):
        for r0 in range(0, cache_ref.shape[0], row_chunk):
            rs = slice(r0, r0 + row_chunk)
            rolled = pltpu.roll(cache_ref[rs, :], length - nq, axis=1)
            if length > LANES:
                out_ref[rs, 0:length - LANES] = rolled[:, 0:length - LANES]
            out_ref[rs, length - LANES:length] = jnp.where(lane < LANES - nq, rolled[:, length - LANES:length],
                                                           new_t[rs, :])

    qa = qa_ref[...]
    kva = kva_ref[...]
    zeros = jnp.zeros((nq, hd), F32)
    qbd_a = jnp.concatenate(
        [jnp.concatenate([qa[:, h * hd:(h + 1) * hd] if g == h // rep else zeros for g in range(KV_A)], axis=1)
         for h in range(H_A)], axis=0).astype(BF16)
    kn_a = new_rows(kva[:, 0:ga])
    vn_a = new_rows(kva[:, ga:2 * ga])
    hrow = lax.broadcasted_iota(jnp.int32, (rows, 1), 0) // nq
    sink = jnp.zeros((rows, 1), F32)
    for h in range(H_A):
        sink = jnp.where(hrow == h, sink_ref[h], sink)
    s_c = _dot(qbd_a, cak_ref[...].astype(BF16)) + bac_ref[...]
    s_n = _dot_nt(qbd_a, kn_a.astype(BF16)) + ban_ref[...]
    m = jnp.maximum(jnp.maximum(jnp.max(s_c, axis=-1, keepdims=True), jnp.max(s_n, axis=-1, keepdims=True)), sink)
    p_c, p_n = jnp.exp(s_c - m), jnp.exp(s_n - m)
    l = jnp.sum(p_c, axis=-1, keepdims=True) + jnp.sum(p_n, axis=-1, keepdims=True) + jnp.exp(sink - m)
    o = (_dot_nt(p_c.astype(BF16), cav_ref[...].astype(BF16)) + _dot(p_n.astype(BF16), vn_a.astype(BF16))) / l
    oa_ref[...] = jnp.concatenate(
        [o[h * nq:(h + 1) * nq, (h // rep) * hd:(h // rep + 1) * hd] for h in range(H_A)], axis=1)
    shifted(cak_ref, nak_ref, kn_a.T, la, ga)
    shifted(cav_ref, nav_ref, vn_a.T, la, ga)

    r = lax.broadcasted_iota(jnp.int32, (rows, HB_W), 0)
    c = lax.broadcasted_iota(jnp.int32, (rows, HB_W), 1)
    own = (r // nq) == (c // hd)
    qbd = jnp.where(own, jnp.concatenate([qb_ref[...]] * H_B, axis=0), 0.0).astype(BF16)
    kn_b = new_rows(kb_ref[...])
    vn_b = new_rows(vb_ref[...])
    chunk = 512
    for c0 in range(0, lb, chunk):
        cs = slice(c0, c0 + chunk)
        s_s[:, cs] = _dot(qbd, cbk_ref[:, cs].astype(BF16)) + bbc_ref[:, cs]
    s_n = _dot_nt(qbd, kn_b.astype(BF16)) + bbn_ref[...]
    s_c = s_s[...]
    m = jnp.maximum(jnp.max(s_c, axis=-1, keepdims=True), jnp.max(s_n, axis=-1, keepdims=True))
    p_c, p_n = jnp.exp(s_c - m), jnp.exp(s_n - m)
    l = jnp.sum(p_c, axis=-1, keepdims=True) + jnp.sum(p_n, axis=-1, keepdims=True)
    p_s[...] = p_c.astype(BF16)
    o = _dot(p_n.astype(BF16), vn_b.astype(BF16))
    for c0 in range(0, lb, chunk):
        cs = slice(c0, c0 + chunk)
        o = o + _dot_nt(p_s[:, cs], cbv_ref[:, cs].astype(BF16))
    o = jnp.where(own, o, 0.0) / l
    out = o[0:nq]
    for h in range(1, H_B):
        out = out + o[h * nq:(h + 1) * nq]
    ob_ref[...] = out
    shifted(cbk_ref, nbk_ref, kn_b.T, lb, 64)
    shifted(cbv_ref, nbv_ref, vn_b.T, lb, 64)


def _attn_sample(layer, depth, sink, qa, qb, kva, kb, vb, cak, cav, cbk, cbv, biases, row0, nb, nq, chain):
    la, lb = cak.shape[3], cbk.shape[3]
    ga = KV_A * HEAD_DIM
    rows = H_A * nq
    blk0 = row0 // nq
    smem = pl.BlockSpec(memory_space=pltpu.SMEM)

    def tok(wd):
        return pl.BlockSpec((nq, wd), lambda b: (blk0 + b, 0))

    def cache(r, length):
        return pl.BlockSpec((None, None, r, length), lambda b: (layer, b, 0, 0))

    def const(a):
        return pl.BlockSpec(a.shape, lambda b: (0, 0))

    in_specs = [smem, tok(QA_W), tok(HB_W), tok(KVA_W), tok(HB_W), tok(HB_W),
                cache(ga, la), cache(ga, la), cache(HB_W, lb), cache(HB_W, lb)] + [const(a) for a in biases]
    args = [sink, qa, qb, kva, kb, vb, cak, cav, cbk, cbv, *biases]
    aliases = {}
    if chain is not None:
        in_specs += [pl.BlockSpec(memory_space=pl.ANY)] * 2
        aliases = {len(args): 4, len(args) + 1: 5}
        args += list(chain)
    out_tok = pl.BlockSpec((nq, QA_W), lambda b: (b, 0))
    return pl.pallas_call(
        functools.partial(_attn_sample_kernel, nq=nq, la=la, lb=lb, chained=chain is not None),
        grid=(nb,),
        in_specs=in_specs,
        out_specs=[out_tok, out_tok,
                   pl.BlockSpec((None, ga, la), lambda b: (b, 0, 0)), pl.BlockSpec((None, ga, la), lambda b: (b, 0, 0)),
                   cache(HB_W, lb), cache(HB_W, lb)],
        out_shape=[jax.ShapeDtypeStruct((nb * nq, QA_W), F32), jax.ShapeDtypeStruct((nb * nq, HB_W), F32),
                   jax.ShapeDtypeStruct((nb, ga, la), F32), jax.ShapeDtypeStruct((nb, ga, la), F32),
                   jax.ShapeDtypeStruct((depth, nb, HB_W, lb), F32), jax.ShapeDtypeStruct((depth, nb, HB_W, lb), F32)],
        scratch_shapes=[pltpu.VMEM((rows, lb), F32), pltpu.VMEM((rows, lb), BF16)],
        input_output_aliases=aliases,
        compiler_params=_params("arbitrary"),
        name="attn_sample",
    )(*args)


def _post_kernel(*refs, alpha, with_router):
    if with_router:
        (x_ref, oa_ref, ob_ref, g_ref, wa_ref, wb_ref, wo_ref, lg_ref, lb_ref, wr_ref,
         x1_ref, x1b_ref, comb_ref) = refs
    else:
        (x_ref, oa_ref, ob_ref, g_ref, wa_ref, wb_ref, wo_ref, lg_ref, lb_ref,
         x1_ref, x1b_ref) = refs
    a = _dot(oa_ref[...], wa_ref[...])
    b = _dot(ob_ref[...], wb_ref[...])
    g = g_ref[...]
    merged = _sigmoid(g[:, :D_MODEL]) * a + _sigmoid(g[:, D_MODEL:]) * b
    y = _dot(merged.astype(BF16), wo_ref[...])
    x1 = _layer_norm(alpha * x_ref[...] + y, lg_ref[...], lb_ref[...])
    x1_ref[...] = x1
    x1b_ref[...] = x1.astype(BF16)
    if with_router:
        logits = jnp.dot(x1, wr_ref[...], preferred_element_type=F32, precision=lax.Precision.HIGHEST)
        lane = lax.broadcasted_iota(jnp.int32, logits.shape, 1)
        logits = jnp.where(lane < N_EXPERTS, logits, -jnp.inf)
        t1 = jnp.max(logits, axis=-1, keepdims=True)
        lane = lane.astype(F32)
        i1 = jnp.min(jnp.where(logits == t1, lane, float(LANES)), axis=-1, keepdims=True)
        rest = jnp.where(lane == i1, -jnp.inf, logits)
        t2 = jnp.max(rest, axis=-1, keepdims=True)
        i2 = jnp.min(jnp.where(rest == t2, lane, float(LANES)), axis=-1, keepdims=True)
        e2 = jnp.exp(t2 - t1)
        den = 1.0 + e2
        comb_ref[...] = jnp.where(lane == i1, 1.0 / den, jnp.where(lane == i2, e2 / den, 0.0))


def _post(x, oa, ob, g, wa, wb, wo, ln_g, ln_b, alpha, w_router=None):
    m = x.shape[0]
    tm = _row_tile(m, 512)
    with_router = w_router is not None

    def rows(wd):
        return pl.BlockSpec((tm, wd), lambda i: (i, 0))

    def const(shape):
        return pl.BlockSpec(shape, lambda i: (0, 0))

    in_specs = [rows(D_MODEL), rows(QA_W), rows(HB_W), rows(2 * D_MODEL),
                const(wa.shape), const(wb.shape), const(wo.shape), const((1, D_MODEL)), const((1, D_MODEL))]
    args = [x, oa, ob, g, wa, wb, wo, ln_g, ln_b]
    out_specs = [rows(D_MODEL), rows(D_MODEL)]
    out_shape = [jax.ShapeDtypeStruct((m, D_MODEL), F32), jax.ShapeDtypeStruct((m, D_MODEL), BF16)]
    if with_router:
        in_specs.append(const(w_router.shape))
        args.append(w_router)
        out_specs.append(rows(LANES))
        out_shape.append(jax.ShapeDtypeStruct((m, LANES), F32))
    return pl.pallas_call(
        functools.partial(_post_kernel, alpha=alpha, with_router=with_router),
        grid=(m // tm,),
        in_specs=in_specs, out_specs=out_specs, out_shape=out_shape,
        compiler_params=_params("arbitrary"),
        name="post_router" if with_router else "post",
    )(*args)


def _swiglu_chunk(xb, wg, wu, wd):
    hg = _dot(xb, wg)
    hu = _dot(xb, wu)
    return _dot((hg * _sigmoid(hg) * hu).astype(BF16), wd)


def _ffn_kernel(x_ref, xb_ref, wg_ref, wu_ref, wd_ref, lg_ref, lb_ref, x2_ref, x2b_ref, acc_ref,
                *, alpha, nf):
    f = pl.program_id(1)

    @pl.when(f == 0)
    def _():
        acc_ref[...] = jnp.zeros(acc_ref.shape, F32)

    acc_ref[...] += _swiglu_chunk(xb_ref[...], wg_ref[...], wu_ref[...], wd_ref[...])

    @pl.when(f == nf - 1)
    def _():
        x2 = _layer_norm(alpha * x_ref[...] + acc_ref[...], lg_ref[...], lb_ref[...])
        x2_ref[...] = x2
        x2b_ref[...] = x2.astype(BF16)


def _ffn(x, xb, wg, wu, wd, ln_g, ln_b, alpha):
    m = x.shape[0]
    tm = _row_tile(m, 512)
    d_ff = wg.shape[1]
    tf = 256
    nf = d_ff // tf
    rows = pl.BlockSpec((tm, D_MODEL), lambda i, f: (i, 0))
    vec = pl.BlockSpec((1, D_MODEL), lambda i, f: (0, 0))
    return pl.pallas_call(
        functools.partial(_ffn_kernel, alpha=alpha, nf=nf),
        grid=(m // tm, nf),
        in_specs=[rows, rows,
                  pl.BlockSpec((D_MODEL, tf), lambda i, f: (0, f)),
                  pl.BlockSpec((D_MODEL, tf), lambda i, f: (0, f)),
                  pl.BlockSpec((tf, D_MODEL), lambda i, f: (f, 0)),
                  vec, vec],
        out_specs=[rows, rows],
        out_shape=[jax.ShapeDtypeStruct((m, D_MODEL), F32), jax.ShapeDtypeStruct((m, D_MODEL), BF16)],
        scratch_shapes=[pltpu.VMEM((tm, D_MODEL), F32)],
        compiler_params=_params("arbitrary", "arbitrary"),
        name="ffn",
    )(x, xb, wg, wu, wd, ln_g, ln_b)


def _moe_kernel(x_ref, xb_ref, comb_ref, wg_ref, wu_ref, wd_ref, lg_ref, lb_ref, x2_ref, x2b_ref, acc_ref,
                *, alpha, nf):
    e = pl.program_id(1)
    f = pl.program_id(2)

    @pl.when((e == 0) & (f == 0))
    def _():
        acc_ref[...] = jnp.zeros(acc_ref.shape, F32)

    comb = comb_ref[...]
    lane = lax.broadcasted_iota(jnp.int32, comb.shape, 1)
    gate = jnp.sum(jnp.where(lane == e, comb, 0.0), axis=-1, keepdims=True)
    acc_ref[...] += gate * _swiglu_chunk(xb_ref[...], wg_ref[...], wu_ref[...], wd_ref[...])

    @pl.when((e == N_EXPERTS - 1) & (f == nf - 1))
    def _():
        x2 = _layer_norm(alpha * x_ref[...] + acc_ref[...], lg_ref[...], lb_ref[...])
        x2_ref[...] = x2
        x2b_ref[...] = x2.astype(BF16)


def _moe(layer_moe, x, xb, comb, wg, wu, wd, ln_g, ln_b, alpha):
    m = x.shape[0]
    tm = _row_tile(m, 512)
    d_e = wg.shape[3]
    tf = 512
    nf = d_e // tf
    rows = pl.BlockSpec((tm, D_MODEL), lambda i, e, f: (i, 0))
    vec = pl.BlockSpec((1, D_MODEL), lambda i, e, f: (0, 0))
    return pl.pallas_call(
        functools.partial(_moe_kernel, alpha=alpha, nf=nf),
        grid=(m // tm, N_EXPERTS, nf),
        in_specs=[rows, rows, pl.BlockSpec((tm, LANES), lambda i, e, f: (i, 0)),
                  pl.BlockSpec((None, None, D_MODEL, tf), lambda i, e, f: (layer_moe, e, 0, f)),
                  pl.BlockSpec((None, None, D_MODEL, tf), lambda i, e, f: (layer_moe, e, 0, f)),
                  pl.BlockSpec((None, None, tf, D_MODEL), lambda i, e, f: (layer_moe, e, f, 0)),
                  vec, vec],
        out_specs=[rows, rows],
        out_shape=[jax.ShapeDtypeStruct((m, D_MODEL), F32), jax.ShapeDtypeStruct((m, D_MODEL), BF16)],
        scratch_shapes=[pltpu.VMEM((tm, D_MODEL), F32)],
        compiler_params=_params("arbitrary", "arbitrary", "arbitrary"),
        name="moe",
    )(x, xb, comb, wg, wu, wd, ln_g, ln_b)


def _seq_minor(cache):
    d, n, length, h, hd = cache.shape
    return jnp.transpose(cache, (0, 1, 3, 4, 2)).reshape(d, n, h * hd, length)


def _seq_major(x, heads):
    d, n, hw, length = x.shape
    return jnp.transpose(x.reshape(d, n, heads, hw // heads, length), (0, 1, 4, 2, 3))


def kernel(x_prompt, x_sample, cache_a_k, cache_a_v, cache_b_k, cache_b_v, w_in, attn_sink, w_branch_a,
           w_branch_b, w_out, ln1_g, ln1_b, ln2_g, ln2_b, ffn_w_gate, ffn_w_up, ffn_w_down, moe_router,
           moe_w_gate, moe_w_up, moe_w_down):
    depth = w_in.shape[0]
    bp, seq, _ = x_prompt.shape
    nb, nq, _ = x_sample.shape
    la, lb = cache_a_k.shape[2], cache_b_k.shape[2]
    assert seq % 512 == 0 and seq <= W_MAX and la == WIN_A and lb == W_MAX and nq == 8
    mp, ms = bp * seq, nb * nq
    alpha = (2.0 * depth) ** 0.25

    n_heads = H_A + H_B
    slopes = jnp.exp2(-8.0 * jnp.arange(1, n_heads + 1, dtype=F32) / n_heads)
    slopes_a, slopes_b = slopes[0::2], slopes[1::2]
    tabs = (_tile_bias(slopes_a, BLK, 2 * BLK, BLK, 1, WIN_A - 1),
            _tile_bias(slopes_b, BLK, 2 * BLK, BLK, DILATED_PAIRS[0][1], DILATED_PAIRS[0][0] // DILATED_PAIRS[0][1]),
            _tile_bias(slopes_b, BLK, 2 * BLK, BLK, DILATED_PAIRS[1][1], DILATED_PAIRS[1][0] // DILATED_PAIRS[1][1]),
            _tile_bias(slopes_b, seq // 16, seq // 16, 0, DILATED_PAIRS[2][1],
                       DILATED_PAIRS[2][0] // DILATED_PAIRS[2][1]))
    biases = _sample_bias(slopes_a, nq, la, WIN_A - 1, False) + _sample_bias(slopes_b, nq, lb, W_MAX, True)

    cak, cav, cbk, cbv = (_seq_minor(c) for c in (cache_a_k, cache_a_v, cache_b_k, cache_b_v))

    w_in_b = w_in.astype(BF16)
    kv0, kv1 = QA_W, QA_W + KVA_W
    kb0 = kv1 + HB_W
    w_in_t = jnp.transpose(jnp.concatenate([w_in_b[:, :, kv0:kv1], w_in_b[:, :, kb0:kb0 + 2 * HB_W]], axis=2),
                           (0, 2, 1))
    wa_b, wb_b, wo_b = w_branch_a.astype(BF16), w_branch_b.astype(BF16), w_out.astype(BF16)
    fg_b, fu_b, fd_b = ffn_w_gate.astype(BF16), ffn_w_up.astype(BF16), ffn_w_down.astype(BF16)
    mg_b, mu_b, md_b = moe_w_gate.astype(BF16), moe_w_up.astype(BF16), moe_w_down.astype(BF16)
    router_pad = jnp.pad(moe_router.astype(F32), ((0, 0), (0, 0), (0, LANES - N_EXPERTS)))

    x = jnp.concatenate([x_prompt.reshape(mp, D_MODEL), x_sample.reshape(ms, D_MODEL)], axis=0)
    xb = x.astype(BF16)

    pak, pav, pbk, pbv, sak, sav = [], [], [], [], [], []
    chain = None
    for l in range(depth):
        qa, kva, qb, kb, vb, g, kat, vat, kbt, vbt = _inproj(xb, w_in_b[l], w_in_t[l], bp, seq)
        oa_p, ob_p = _attn_prompt(attn_sink[l], qa, qb, kva, kb, vb, tabs, bp, seq)
        oa_s, ob_s, nak, nav, nbk, nbv = _attn_sample(l, depth, attn_sink[l], qa, qb, kva, kb, vb,
                                                      cak, cav, cbk, cbv, biases, mp, nb, nq, chain)
        chain = (nbk, nbv)
        oa = jnp.concatenate([oa_p, oa_s.astype(BF16)], axis=0)
        ob = jnp.concatenate([ob_p, ob_s.astype(BF16)], axis=0)
        pak.append(kat[:, :, seq - WIN_A:])
        pav.append(vat[:, :, seq - WIN_A:])
        pbk.append(kbt)
        pbv.append(vbt)
        sak.append(nak)
        sav.append(nav)

        ln1 = (ln1_g[l].reshape(1, D_MODEL), ln1_b[l].reshape(1, D_MODEL))
        ln2 = (ln2_g[l].reshape(1, D_MODEL), ln2_b[l].reshape(1, D_MODEL))
        i = l // 2
        if l % 2 == 0:
            x1, x1b = _post(x, oa, ob, g, wa_b[l], wb_b[l], wo_b[l], *ln1, alpha)
            x, xb = _ffn(x1, x1b, fg_b[i], fu_b[i], fd_b[i], *ln2, alpha)
        else:
            x1, x1b, comb = _post(x, oa, ob, g, wa_b[l], wb_b[l], wo_b[l], *ln1, alpha, router_pad[i])
            x, xb = _moe(i, x1, x1b, comb, mg_b, mu_b, md_b, *ln2, alpha)

    return (x[:mp].reshape(bp, seq, D_MODEL), x[mp:].reshape(nb, nq, D_MODEL),
            _seq_major(jnp.stack(pak), KV_A), _seq_major(jnp.stack(pav), KV_A),
            _seq_major(jnp.stack(pbk), H_B), _seq_major(jnp.stack(pbv), H_B),
            _seq_major(jnp.stack(sak), KV_A), _seq_major(jnp.stack(sav), KV_A),
            _seq_major(chain[0], H_B), _seq_major(chain[1], H_B))
```

```python
import functools
import math

import jax
import jax.numpy as jnp
from jax import lax
from jax.experimental import pallas as pl
from jax.experimental.pallas import tpu as pltpu

F32 = jnp.float32
BF16 = jnp.bfloat16

D_MODEL = 1024
HEAD_DIM = 64
H_A = 8
KV_A = 2
WIN_A = 128
H_B = 8
DILATED_PAIRS = ((128, 1), (512, 4), (2048, 16))
W_MAX = 2048
N_EXPERTS = 8
LN_EPS = 1e-5
NEG_INF = -1e30
QA_W = H_A * HEAD_DIM
KVA_W = 2 * KV_A * HEAD_DIM
HB_W = H_B * HEAD_DIM
IN_COLS = QA_W + KVA_W + 3 * HB_W + 2 * D_MODEL
LANES = 128
VMEM_LIMIT_BYTES = 56 * 1024 * 1024


def _params(*sem):
    return pltpu.CompilerParams(dimension_semantics=sem, vmem_limit_bytes=VMEM_LIMIT_BYTES)


def _row_tile(m, cap):
    t = cap
    while m % t:
        t //= 2
    assert t >= 8, (m, cap)
    return t


def _sigmoid(x):
    return 1.0 / (1.0 + jnp.exp(-x))


def _layer_norm(z, g, b):
    mu = jnp.mean(z, axis=-1, keepdims=True)
    zc = z - mu
    var = jnp.mean(zc * zc, axis=-1, keepdims=True)
    return zc * lax.rsqrt(var + LN_EPS) * g + b


def _dot(a, b):
    return jnp.dot(a, b, preferred_element_type=F32)


def _dot_nt(a, b):
    return lax.dot_general(a, b, (((1,), (1,)), ((), ())), preferred_element_type=F32)


KVT_ROWS = 2 * KV_A * HEAD_DIM + 2 * HB_W


def _inproj_kernel(x_ref, w_ref, wt_ref, qa_ref, kva_ref, qb_ref, kb_ref, vb_ref, g_ref,
                   kat_ref, vat_ref, kbt_ref, vbt_ref, *, prompt_tiles):
    x = x_ref[...]
    scale = HEAD_DIM ** -0.5

    def mm(c0, c1):
        return _dot(x, w_ref[:, c0:c1])

    c = 0
    qa_ref[...] = mm(c, c + QA_W) * scale
    c += QA_W
    kva_ref[...] = mm(c, c + KVA_W)
    c += KVA_W
    qb_ref[...] = mm(c, c + HB_W) * scale
    c += HB_W
    kb_ref[...] = mm(c, c + HB_W)
    c += HB_W
    vb_ref[...] = mm(c, c + HB_W)
    c += HB_W
    g_ref[...] = mm(c, c + 2 * D_MODEL)

    @pl.when(pl.program_id(0) < prompt_tiles)
    def _seq_minor_outputs():
        def mt(r0, r1):
            return _dot_nt(wt_ref[r0:r1, :], x)
        ga = KV_A * HEAD_DIM
        kat_ref[...] = mt(0, ga)
        vat_ref[...] = mt(ga, 2 * ga)
        kbt_ref[...] = mt(2 * ga, 2 * ga + HB_W)
        vbt_ref[...] = mt(2 * ga + HB_W, 2 * ga + 2 * HB_W)


def _inproj(xb, w, wt, bp, seq):
    m = xb.shape[0]
    tm = _row_tile(math.gcd(m, seq), 512)
    tiles_per_seq = seq // tm
    prompt_tiles = bp * tiles_per_seq
    widths = (QA_W, KVA_W, HB_W, HB_W, HB_W, 2 * D_MODEL)
    ga = KV_A * HEAD_DIM

    def seq_minor(rows):
        def index(i):
            j = jnp.minimum(i, prompt_tiles - 1)
            return (j // tiles_per_seq, 0, j % tiles_per_seq)
        return pl.BlockSpec((None, rows, tm), index)

    t_rows = (ga, ga, HB_W, HB_W)
    return pl.pallas_call(
        functools.partial(_inproj_kernel, prompt_tiles=prompt_tiles),
        grid=(m // tm,),
        in_specs=[pl.BlockSpec((tm, D_MODEL), lambda i: (i, 0)),
                  pl.BlockSpec((D_MODEL, IN_COLS), lambda i: (0, 0)),
                  pl.BlockSpec((KVT_ROWS, D_MODEL), lambda i: (0, 0))],
        out_specs=[pl.BlockSpec((tm, wd), lambda i: (i, 0)) for wd in widths] + [seq_minor(r) for r in t_rows],
        out_shape=[jax.ShapeDtypeStruct((m, wd), F32) for wd in widths]
        + [jax.ShapeDtypeStruct((bp, r, seq), F32) for r in t_rows],
        compiler_params=_params("arbitrary"),
        name="inproj",
    )(xb, w, wt)


BLK = 128
HEADS_PER_STEP = 2


def _tile_bias(slopes, n_q, n_k, shift, dist_scale, max_sub):
    r = jnp.arange(n_q, dtype=jnp.int32)[:, None]
    c = jnp.arange(n_k, dtype=jnp.int32)[None, :]
    dsub = shift + r - c
    ok = (dsub >= 0) & (dsub <= max_sub)
    b = -slopes[:, None, None] * (dist_scale * dsub).astype(F32)[None]
    full = jnp.where(ok[None], b, NEG_INF)
    first = jnp.where((ok & (c >= shift))[None], b, NEG_INF)
    return jnp.stack([full, first], axis=1).astype(F32)


def _softmax_tile(q, k, v, bias, sink=None):
    s = _dot_nt(q, k) + bias
    m = jnp.max(s, axis=-1, keepdims=True)
    if sink is not None:
        m = jnp.maximum(m, sink)
    p = jnp.exp(s - m)
    l = jnp.sum(p, axis=-1, keepdims=True)
    if sink is not None:
        l = l + jnp.exp(sink - m)
    o = _dot(p.astype(BF16), v) / l
    return o, m + jnp.log(l)


def _attn_prompt_kernel(sink_ref, qa_ref, qb_ref, kva_ref, kb_ref, vb_ref, ta_ref, t1_ref, t2_ref, t3_ref,
                        oa_ref, ob_ref,
                        qa_s, ka_s, va_s, q1_s, k1_s, v1_s, q4_s, k4_s, v4_s, q16_s, k16_s, v16_s, ro_s, rl_s,
                        *, seq):
    hp = pl.program_id(1)
    nblk = seq // BLK
    n4, n16 = seq // 4, seq // 16
    sub4 = n4 + BLK
    hd = HEAD_DIM

    @pl.when((pl.program_id(0) == 0) & (hp == 0))
    def _zero_key_pads():
        for ref in (ka_s, va_s, k1_s, v1_s, k4_s, v4_s):
            ref[...] = jnp.zeros(ref.shape, BF16)

    def heads(x):
        return [x[:, hh * hd:(hh + 1) * hd].astype(BF16) for hh in range(HEADS_PER_STEP)]

    group0 = (hp // (H_A // KV_A // HEADS_PER_STEP)) == 0

    def stage_natural(c, carry):
        r0 = pl.multiple_of(c * BLK, BLK)
        rows, prows = pl.ds(r0, BLK), pl.ds(r0 + BLK, BLK)
        kva = kva_ref[rows, :]
        ka_s[prows, :] = jnp.where(group0, kva[:, 0:hd], kva[:, hd:2 * hd]).astype(BF16)
        va_s[prows, :] = jnp.where(group0, kva[:, 2 * hd:3 * hd], kva[:, 3 * hd:4 * hd]).astype(BF16)
        for hh, (qa, qb, kb, vb) in enumerate(zip(heads(qa_ref[rows, :]), heads(qb_ref[rows, :]),
                                                  heads(kb_ref[rows, :]), heads(vb_ref[rows, :]))):
            qa_s[hh, rows, :] = qa
            q1_s[hh, rows, :] = qb
            k1_s[hh, prows, :] = kb
            v1_s[hh, prows, :] = vb
        return carry
    lax.fori_loop(0, nblk, stage_natural, 0, unroll=4)

    for r4 in range(4):
        def stage_mod4(c, carry, r4=r4):
            j0 = pl.multiple_of(c * BLK, BLK)
            src = pl.ds(r4 + 4 * j0, BLK, stride=4)
            for hh, (qb, kb, vb) in enumerate(zip(heads(qb_ref[src, :]), heads(kb_ref[src, :]),
                                                  heads(vb_ref[src, :]))):
                q4_s[hh, pl.ds(r4 * n4 + j0, BLK), :] = qb
                k4_s[hh, pl.ds(r4 * sub4 + BLK + j0, BLK), :] = kb
                v4_s[hh, pl.ds(r4 * sub4 + BLK + j0, BLK), :] = vb
            return carry
        lax.fori_loop(0, n4 // BLK, stage_mod4, 0, unroll=True)

    def stage_mod16(r, carry):
        src = pl.ds(r, n16, stride=16)
        dst = pl.ds(pl.multiple_of(r * n16, n16), n16)
        for hh, (qb, kb, vb) in enumerate(zip(heads(qb_ref[src, :]), heads(kb_ref[src, :]),
                                              heads(vb_ref[src, :]))):
            q16_s[hh, dst, :] = qb
            k16_s[hh, dst, :] = kb
            v16_s[hh, dst, :] = vb
        return carry
    lax.fori_loop(0, 16, stage_mod16, 0, unroll=4)

    def mixer_a(i, carry):
        r0 = pl.multiple_of(i * BLK, BLK)
        first = jnp.where(i == 0, 1, 0)
        k = ka_s[pl.ds(r0, 2 * BLK), :]
        v = va_s[pl.ds(r0, 2 * BLK), :]
        outs = []
        for hh in range(HEADS_PER_STEP):
            o, _ = _softmax_tile(qa_s[hh, pl.ds(r0, BLK), :], k, v, ta_ref[hh, first],
                                 sink_ref[HEADS_PER_STEP * hp + hh])
            outs.append(o)
        oa_ref[pl.ds(r0, BLK), :] = jnp.concatenate(outs, axis=1).astype(BF16)
        return carry
    lax.fori_loop(0, nblk, mixer_a, 0, unroll=True)

    def keep(pair, dst, o, lse):
        ro_s[pair, dst, :] = o
        rl_s[pair, dst, :] = jnp.broadcast_to(lse, o.shape)

    for hh in range(HEADS_PER_STEP):
        def pair_w128(i, carry, hh=hh):
            r0 = pl.multiple_of(i * BLK, BLK)
            o, lse = _softmax_tile(q1_s[hh, pl.ds(r0, BLK), :], k1_s[hh, pl.ds(r0, 2 * BLK), :],
                                   v1_s[hh, pl.ds(r0, 2 * BLK), :], t1_ref[hh, jnp.where(i == 0, 1, 0)])
            keep(0, pl.ds(r0, BLK), o, lse)
            return carry
        lax.fori_loop(0, nblk, pair_w128, 0, unroll=True)

        for r4 in range(4):
            def pair_w512(i, carry, hh=hh, r4=r4):
                j0 = pl.multiple_of(i * BLK, BLK)
                o, lse = _softmax_tile(q4_s[hh, pl.ds(r4 * n4 + j0, BLK), :],
                                       k4_s[hh, pl.ds(r4 * sub4 + j0, 2 * BLK), :],
                                       v4_s[hh, pl.ds(r4 * sub4 + j0, 2 * BLK), :],
                                       t2_ref[hh, jnp.where(i == 0, 1, 0)])
                keep(1, pl.ds(r4 + 4 * j0, BLK, stride=4), o, lse)
                return carry
            lax.fori_loop(0, n4 // BLK, pair_w512, 0, unroll=True)

        def pair_w2048(r, carry, hh=hh):
            src = pl.ds(pl.multiple_of(r * n16, n16), n16)
            o, lse = _softmax_tile(q16_s[hh, src, :], k16_s[hh, src, :], v16_s[hh, src, :], t3_ref[hh, 0])
            keep(2, pl.ds(r, n16, stride=16), o, lse)
            return carry
        lax.fori_loop(0, 16, pair_w2048, 0, unroll=True)

        def merge(i, carry, hh=hh):
            rows = pl.ds(pl.multiple_of(i * BLK, BLK), BLK)
            l0, l1, l2 = rl_s[0, rows, :], rl_s[1, rows, :], rl_s[2, rows, :]
            mx = jnp.maximum(l0, jnp.maximum(l1, l2))
            w0, w1, w2 = jnp.exp(l0 - mx), jnp.exp(l1 - mx), jnp.exp(l2 - mx)
            o = (w0 * ro_s[0, rows, :] + w1 * ro_s[1, rows, :] + w2 * ro_s[2, rows, :]) / (w0 + w1 + w2)
            ob_ref[rows, hh * hd:(hh + 1) * hd] = o.astype(BF16)
            return carry
        lax.fori_loop(0, nblk, merge, 0, unroll=4)


def _attn_prompt(sink, qa, qb, kva, kb, vb, tabs, batch, seq):
    mp = batch * seq
    n4 = seq // 4
    lanes = HEADS_PER_STEP * HEAD_DIM
    smem = pl.BlockSpec(memory_space=pltpu.SMEM)
    cols = pl.BlockSpec((seq, lanes), lambda b, hp: (b, hp))

    def tab(t):
        return pl.BlockSpec((HEADS_PER_STEP,) + t.shape[1:], lambda b, hp: (hp, 0, 0, 0))

    def per_head(rows):
        return pltpu.VMEM((HEADS_PER_STEP, rows, HEAD_DIM), BF16)

    return pl.pallas_call(
        functools.partial(_attn_prompt_kernel, seq=seq),
        grid=(batch, H_B // HEADS_PER_STEP),
        in_specs=[smem, cols, cols, pl.BlockSpec((seq, KVA_W), lambda b, hp: (b, 0)), cols, cols]
        + [tab(t) for t in tabs],
        out_specs=[cols, cols],
        out_shape=[jax.ShapeDtypeStruct((mp, QA_W), BF16), jax.ShapeDtypeStruct((mp, HB_W), BF16)],
        scratch_shapes=[per_head(seq), pltpu.VMEM((seq + BLK, HEAD_DIM), BF16), pltpu.VMEM((seq + BLK, HEAD_DIM), BF16),
                        per_head(seq), per_head(seq + BLK), per_head(seq + BLK),
                        per_head(seq), per_head(4 * (n4 + BLK)), per_head(4 * (n4 + BLK)),
                        per_head(seq), per_head(seq), per_head(seq),
                        pltpu.VMEM((3, seq, HEAD_DIM), F32), pltpu.VMEM((3, seq, HEAD_DIM), F32)],
        compiler_params=_params("arbitrary", "arbitrary"),
        name="attn_prompt",
    )(sink, qa, qb, kva, kb, vb, *tabs)


def _sample_bias(slopes, nq, cache_len, max_dist, dilated):
    i = jnp.arange(nq, dtype=jnp.int32)[:, None]

    def make(dist, real):
        if dilated:
            cnt = jnp.zeros(dist.shape, jnp.int32)
            for w, d in DILATED_PAIRS:
                cnt = cnt + jnp.where(((dist & (d - 1)) == 0) & (dist <= w), 1, 0)
            ok = real & (dist >= 0) & (cnt > 0)
            extra = jnp.where(cnt == 3, math.log(3.0), jnp.where(cnt == 2, math.log(2.0), 0.0)).astype(F32)
        else:
            ok = real & (dist >= 0) & (dist <= max_dist)
            extra = jnp.zeros(dist.shape, F32)
        bias = -slopes[:, None, None] * dist.astype(F32)[None] + extra[None]
        return jnp.where(ok[None], bias, NEG_INF).reshape(slopes.shape[0] * nq, dist.shape[1]).astype(F32)

    key = jnp.arange(cache_len, dtype=jnp.int32)[None, :]
    new = jnp.arange(LANES, dtype=jnp.int32)[None, :] - (LANES - nq)
    return make(cache_len + i - key, key >= 0), make(i - new, new >= 0)


def _attn_sample_kernel(*refs, nq, la, lb, chained):
    (sink_ref, qa_ref, qb_ref, kva_ref, kb_ref, vb_ref, cak_ref, cav_ref, cbk_ref, cbv_ref,
     bac_ref, ban_ref, bbc_ref, bbn_ref) = refs[:14]
    refs = refs[14 + (2 if chained else 0):]
    oa_ref, ob_ref, nak_ref, nav_ref, nbk_ref, nbv_ref, s_s, p_s = refs
    rows = H_A * nq
    hd = HEAD_DIM
    ga = KV_A * hd
    rep = H_A // KV_A

    def new_rows(x):
        return jnp.concatenate([jnp.zeros((LANES - nq, x.shape[1]), F32), x], axis=0)

    lane = lax.broadcasted_iota(jnp.int32, (1, LANES), 1)

    def shifted(cache_ref, out_ref, new_t, length, row_chunk):
        for r0 in range(0, cache_ref.shape[0], row_chunk):
            rs = slice(r0, r0 + row_chunk)
            rolled = pltpu.roll(cache_ref[rs, :], length - nq, axis=1)
            if length > LANES:
                out_ref[rs, 0:length - LANES] = rolled[:, 0:length - LANES]
            out_ref[rs, length - LANES:length] = jnp.where(lane < LANES - nq, rolled[:, length - LANES:length],
                                                           new_t[rs, :])

    qa = qa_ref[...]
    kva = kva_ref[...]
    zeros = jnp.zeros((nq, hd), F32)
    qbd_a = jnp.concatenate(
        [jnp.concatenate([qa[:, h * hd:(h + 1) * hd] if g == h // rep else zeros for g in range(KV_A)], axis=1)
         for h in range(H_A)], axis=0).astype(BF16)
    kn_a = new_rows(kva[:, 0:ga])
    vn_a = new_rows(kva[:, ga:2 * ga])
    hrow = lax.broadcasted_iota(jnp.int32, (rows, 1), 0) // nq
    sink = jnp.zeros((rows, 1), F32)
    for h in range(H_A):
        sink = jnp.where(hrow == h, sink_ref[h], sink)
    s_c = _dot(qbd_a, cak_ref[...].astype(BF16)) + bac_ref[...]
    s_n = _dot_nt(qbd_a, kn_a.astype(BF16)) + ban_ref[...]
    m = jnp.maximum(jnp.maximum(jnp.max(s_c, axis=-1, keepdims=True), jnp.max(s_n, axis=-1, keepdims=True)), sink)
    p_c, p_n = jnp.exp(s_c - m), jnp.exp(s_n - m)
    l = jnp.sum(p_c, axis=-1, keepdims=True) + jnp.sum(p_n, axis=-1, keepdims=True) + jnp.exp(sink - m)
    o = (_dot_nt(p_c.astype(BF16), cav_ref[...].astype(BF16)) + _dot(p_n.astype(BF16), vn_a.astype(BF16))) / l
    oa_ref[...] = jnp.concatenate(
        [o[h * nq:(h + 1) * nq, (h // rep) * hd:(h // rep + 1) * hd] for h in range(H_A)], axis=1)
    shifted(cak_ref, nak_ref, kn_a.T, la, ga)
    shifted(cav_ref, nav_ref, vn_a.T, la, ga)

    r = lax.broadcasted_iota(jnp.int32, (rows, HB_W), 0)
    c = lax.broadcasted_iota(jnp.int32, (rows, HB_W), 1)
    own = (r // nq) == (c // hd)
    qbd = jnp.where(own, jnp.concatenate([qb_ref[...]] * H_B, axis=0), 0.0).astype(BF16)
    kn_b = new_rows(kb_ref[...])
    vn_b = new_rows(vb_ref[...])
    chunk = 512
    for c0 in range(0, lb, chunk):
        cs = slice(c0, c0 + chunk)
        s_s[:, cs] = _dot(qbd, cbk_ref[:, cs].astype(BF16)) + bbc_ref[:, cs]
    s_n = _dot_nt(qbd, kn_b.astype(BF16)) + bbn_ref[...]
    s_c = s_s[...]
    m = jnp.maximum(jnp.max(s_c, axis=-1, keepdims=True), jnp.max(s_n, axis=-1, keepdims=True))
    p_c, p_n = jnp.exp(s_c - m), jnp.exp(s_n - m)
    l = jnp.sum(p_c, axis=-1, keepdims=True) + jnp.sum(p_n, axis=-1, keepdims=True)
    p_s[...] = p_c.astype(BF16)
    o = _dot(p_n.astype(BF16), vn_b.astype(BF16))
    for c0 in range(0, lb, chunk):
        cs = slice(c0, c0 + chunk)
        o = o + _dot_nt(p_s[:, cs], cbv_ref[:, cs].astype(BF16))
    o = jnp.where(own, o, 0.0) / l
    out = o[0:nq]
    for h in range(1, H_B):
        out = out + o[h * nq:(h + 1) * nq]
    ob_ref[...] = out
    shifted(cbk_ref, nbk_ref, kn_b.T, lb, 64)
    shifted(cbv_ref, nbv_ref, vn_b.T, lb, 64)


def _attn_sample(layer, depth, sink, qa, qb, kva, kb, vb, cak, cav, cbk, cbv, biases, row0, nb, nq, chain):
    la, lb = cak.shape[3], cbk.shape[3]
    ga = KV_A * HEAD_DIM
    rows = H_A * nq
    blk0 = row0 // nq
    smem = pl.BlockSpec(memory_space=pltpu.SMEM)

    def tok(wd):
        return pl.BlockSpec((nq, wd), lambda b: (blk0 + b, 0))

    def cache(r, length):
        return pl.BlockSpec((None, None, r, length), lambda b: (layer, b, 0, 0))

    def const(a):
        return pl.BlockSpec(a.shape, lambda b: (0, 0))

    in_specs = [smem, tok(QA_W), tok(HB_W), tok(KVA_W), tok(HB_W), tok(HB_W),
                cache(ga, la), cache(ga, la), cache(HB_W, lb), cache(HB_W, lb)] + [const(a) for a in biases]
    args = [sink, qa, qb, kva, kb, vb, cak, cav, cbk, cbv, *biases]
    aliases = {}
    if chain is not None:
        in_specs += [pl.BlockSpec(memory_space=pl.ANY)] * 2
        aliases = {len(args): 4, len(args) + 1: 5}
        args += list(chain)
    out_tok = pl.BlockSpec((nq, QA_W), lambda b: (b, 0))
    return pl.pallas_call(
        functools.partial(_attn_sample_kernel, nq=nq, la=la, lb=lb, chained=chain is not None),
        grid=(nb,),
        in_specs=in_specs,
        out_specs=[out_tok, out_tok,
                   pl.BlockSpec((None, ga, la), lambda b: (b, 0, 0)), pl.BlockSpec((None, ga, la), lambda b: (b, 0, 0)),
                   cache(HB_W, lb), cache(HB_W, lb)],
        out_shape=[jax.ShapeDtypeStruct((nb * nq, QA_W), F32), jax.ShapeDtypeStruct((nb * nq, HB_W), F32),
                   jax.ShapeDtypeStruct((nb, ga, la), F32), jax.ShapeDtypeStruct((nb, ga, la), F32),
                   jax.ShapeDtypeStruct((depth, nb, HB_W, lb), F32), jax.ShapeDtypeStruct((depth, nb, HB_W, lb), F32)],
        scratch_shapes=[pltpu.VMEM((rows, lb), F32), pltpu.VMEM((rows, lb), BF16)],
        input_output_aliases=aliases,
        compiler_params=_params("arbitrary"),
        name="attn_sample",
    )(*args)


def _post_kernel(*refs, alpha, with_router):
    if with_router:
        (x_ref, oa_ref, ob_ref, g_ref, wa_ref, wb_ref, wo_ref, lg_ref, lb_ref, wr_ref,
         x1_ref, x1b_ref, comb_ref) = refs
    else:
        (x_ref, oa_ref, ob_ref, g_ref, wa_ref, wb_ref, wo_ref, lg_ref, lb_ref,
         x1_ref, x1b_ref) = refs
    a = _dot(oa_ref[...], wa_ref[...])
    b = _dot(ob_ref[...], wb_ref[...])
    g = g_ref[...]
    merged = _sigmoid(g[:, :D_MODEL]) * a + _sigmoid(g[:, D_MODEL:]) * b
    y = _dot(merged.astype(BF16), wo_ref[...])
    x1 = _layer_norm(alpha * x_ref[...] + y, lg_ref[...], lb_ref[...])
    x1_ref[...] = x1
    x1b_ref[...] = x1.astype(BF16)
    if with_router:
        logits = jnp.dot(x1, wr_ref[...], preferred_element_type=F32, precision=lax.Precision.HIGHEST)
        lane = lax.broadcasted_iota(jnp.int32, logits.shape, 1)
        logits = jnp.where(lane < N_EXPERTS, logits, -jnp.inf)
        t1 = jnp.max(logits, axis=-1, keepdims=True)
        lane = lane.astype(F32)
        i1 = jnp.min(jnp.where(logits == t1, lane, float(LANES)), axis=-1, keepdims=True)
        rest = jnp.where(lane == i1, -jnp.inf, logits)
        t2 = jnp.max(rest, axis=-1, keepdims=True)
        i2 = jnp.min(jnp.where(rest == t2, lane, float(LANES)), axis=-1, keepdims=True)
        e2 = jnp.exp(t2 - t1)
        den = 1.0 + e2
        comb_ref[...] = jnp.where(lane == i1, 1.0 / den, jnp.where(lane == i2, e2 / den, 0.0))


def _post(x, oa, ob, g, wa, wb, wo, ln_g, ln_b, alpha, w_router=None):
    m = x.shape[0]
    tm = _row_tile(m, 512)
    with_router = w_router is not None

    def rows(wd):
        return pl.BlockSpec((tm, wd), lambda i: (i, 0))

    def const(shape):
        return pl.BlockSpec(shape, lambda i: (0, 0))

    in_specs = [rows(D_MODEL), rows(QA_W), rows(HB_W), rows(2 * D_MODEL),
                const(wa.shape), const(wb.shape), const(wo.shape), const((1, D_MODEL)), const((1, D_MODEL))]
    args = [x, oa, ob, g, wa, wb, wo, ln_g, ln_b]
    out_specs = [rows(D_MODEL), rows(D_MODEL)]
    out_shape = [jax.ShapeDtypeStruct((m, D_MODEL), F32), jax.ShapeDtypeStruct((m, D_MODEL), BF16)]
    if with_router:
        in_specs.append(const(w_router.shape))
        args.append(w_router)
        out_specs.append(rows(LANES))
        out_shape.append(jax.ShapeDtypeStruct((m, LANES), F32))
    return pl.pallas_call(
        functools.partial(_post_kernel, alpha=alpha, with_router=with_router),
        grid=(m // tm,),
        in_specs=in_specs, out_specs=out_specs, out_shape=out_shape,
        compiler_params=_params("arbitrary"),
        name="post_router" if with_router else "post",
    )(*args)


def _swiglu_chunk(xb, wg, wu, wd):
    hg = _dot(xb, wg)
    hu = _dot(xb, wu)
    return _dot((hg * _sigmoid(hg) * hu).astype(BF16), wd)


def _ffn_kernel(x_ref, xb_ref, wg_ref, wu_ref, wd_ref, lg_ref, lb_ref, x2_ref, x2b_ref, acc_ref,
                *, alpha, nf):
    f = pl.program_id(1)

    @pl.when(f == 0)
    def _():
        acc_ref[...] = jnp.zeros(acc_ref.shape, F32)

    acc_ref[...] += _swiglu_chunk(xb_ref[...], wg_ref[...], wu_ref[...], wd_ref[...])

    @pl.when(f == nf - 1)
    def _():
        x2 = _layer_norm(alpha * x_ref[...] + acc_ref[...], lg_ref[...], lb_ref[...])
        x2_ref[...] = x2
        x2b_ref[...] = x2.astype(BF16)


def _ffn(x, xb, wg, wu, wd, ln_g, ln_b, alpha):
    m = x.shape[0]
    tm = _row_tile(m, 512)
    d_ff = wg.shape[1]
    nf = 2
    tf = d_ff // nf
    rows = pl.BlockSpec((tm, D_MODEL), lambda i, f: (i, 0))
    vec = pl.BlockSpec((1, D_MODEL), lambda i, f: (0, 0))
    return pl.pallas_call(
        functools.partial(_ffn_kernel, alpha=alpha, nf=nf),
        grid=(m // tm, nf),
        in_specs=[rows, rows,
                  pl.BlockSpec((D_MODEL, tf), lambda i, f: (0, f)),
                  pl.BlockSpec((D_MODEL, tf), lambda i, f: (0, f)),
                  pl.BlockSpec((tf, D_MODEL), lambda i, f: (f, 0)),
                  vec, vec],
        out_specs=[rows, rows],
        out_shape=[jax.ShapeDtypeStruct((m, D_MODEL), F32), jax.ShapeDtypeStruct((m, D_MODEL), BF16)],
        scratch_shapes=[pltpu.VMEM((tm, D_MODEL), F32)],
        compiler_params=_params("arbitrary", "arbitrary"),
        name="ffn",
    )(x, xb, wg, wu, wd, ln_g, ln_b)


MOE_TILE = 512
FLAG_FIRST, FLAG_LAST, FLAG_ACTIVE = 1, 2, 4


def _steps(n, lo, total_steps):
    end = jnp.cumsum(n)
    start = end - n
    s = jnp.arange(total_steps, dtype=jnp.int32)
    sc = jnp.minimum(s, end[-1] - 1)
    g = jnp.searchsorted(end, sc, side="right").astype(jnp.int32)
    return g, (lo[g] + sc - start[g]).astype(jnp.int32), sc, s < end[-1], start, end


def _step_flags(active, first, last):
    return (jnp.where(active & first, FLAG_FIRST, 0) + jnp.where(active & last, FLAG_LAST, 0)
            + jnp.where(active, FLAG_ACTIVE, 0)).astype(jnp.int32)


def _route_plan(comb, tm):
    m = comb.shape[0]
    onehot = comb[:, :N_EXPERTS] > 0.0
    oh = onehot.astype(jnp.int32)
    csum = jnp.cumsum(oh, axis=0)
    cnt = csum[-1]
    tiles_e = (cnt + tm - 1) // tm
    tile_end = jnp.cumsum(tiles_e)
    tile_off = tile_end - tiles_e
    pos = jnp.where(onehot, (tile_off * tm)[None, :] + csum - oh, -1)
    nt = 2 * m // tm + N_EXPERTS
    n_active = tile_end[-1].astype(jnp.int32)
    tiles = jnp.arange(nt, dtype=jnp.int32)
    tile_expert = jnp.minimum(jnp.searchsorted(tile_end, tiles, side="right"), N_EXPERTS - 1).astype(jnp.int32)

    k0 = (tiles - tile_off[tile_expert]) * tm
    k1 = jnp.minimum(k0 + tm, cnt[tile_expert]) - 1
    find = jax.vmap(lambda col, q: jnp.searchsorted(col, q, side="left"))

    def token_of_rank(q):
        every = find(csum.T, jnp.broadcast_to(q[None, :], (N_EXPERTS, nt)))
        return jnp.take_along_axis(every, tile_expert[None, :], axis=0)[0]

    t_lo, t_hi = token_of_rank(k0 + 1), token_of_rank(k1 + 1)
    c_lo = (t_lo // tm).astype(jnp.int32)
    n = jnp.where(tiles < n_active, t_hi // tm - t_lo // tm + 1, 0).astype(jnp.int32)
    g, chunk, sc, active, start, end = _steps(n, c_lo, nt + N_EXPERTS * (m // tm))
    dispatch = (g, tile_expert[g], chunk, _step_flags(active, sc == start[g], sc == end[g] - 1))

    nti = m // tm
    posr = pos.reshape(nti, tm, N_EXPERTS)
    p_hi = jnp.max(posr, axis=1)
    p_lo = jnp.min(jnp.where(posr >= 0, posr, nt * tm), axis=1)
    c_lo = (p_lo // tm).reshape(-1).astype(jnp.int32)
    n = jnp.where(p_hi >= 0, p_hi // tm - p_lo // tm + 1, 0).reshape(-1).astype(jnp.int32)
    g, chunk, sc, active, start, end = _steps(n, c_lo, nti * N_EXPERTS + nt)
    ti, ex = g // N_EXPERTS, g % N_EXPERTS
    t_start = start.reshape(nti, N_EXPERTS)[:, 0]
    t_end = end.reshape(nti, N_EXPERTS)[:, -1]
    combine = (ti.astype(jnp.int32), ex.astype(jnp.int32), chunk,
               _step_flags(active, sc == t_start[ti], sc == t_end[ti] - 1))
    return pos.astype(F32), tile_expert, n_active.reshape(1), dispatch, combine


def _accumulate(flags, acc_ref, v):
    @pl.when((flags & FLAG_FIRST) != 0)
    def _():
        acc_ref[...] = v

    @pl.when((flags & FLAG_FIRST) == 0)
    def _():
        acc_ref[...] += v


def _dispatch_kernel(tile_ref, ex_ref, chunk_ref, flag_ref, pos_ref, x_ref, o_ref, acc_ref):
    s = pl.program_id(0)
    flags = flag_ref[s]

    @pl.when((flags & FLAG_ACTIVE) != 0)
    def _():
        tm, tc = acc_ref.shape[0], x_ref.shape[0]
        row = (lax.broadcasted_iota(jnp.int32, (tm, tc), 0) + tile_ref[s] * tm).astype(F32)
        pick = jnp.where(pos_ref[pl.ds(ex_ref[s], 1), :] == row, 1.0, 0.0).astype(BF16)
        _accumulate(flags, acc_ref, _dot(pick, x_ref[...]))

        @pl.when((flags & FLAG_LAST) != 0)
        def _():
            o_ref[...] = acc_ref[...].astype(BF16)


def _dispatch(xb, pos_t, plan, n_tiles, tm):
    tile, ex, chunk, flags = plan
    return pl.pallas_call(
        _dispatch_kernel,
        grid_spec=pltpu.PrefetchScalarGridSpec(
            num_scalar_prefetch=4,
            grid=(tile.shape[0],),
            in_specs=[pl.BlockSpec((N_EXPERTS, tm), lambda s, t, e, c, f: (0, c[s])),
                      pl.BlockSpec((tm, D_MODEL), lambda s, t, e, c, f: (c[s], 0))],
            out_specs=pl.BlockSpec((tm, D_MODEL), lambda s, t, e, c, f: (t[s], 0)),
            scratch_shapes=[pltpu.VMEM((tm, D_MODEL), F32)]),
        out_shape=jax.ShapeDtypeStruct((n_tiles * tm, D_MODEL), BF16),
        compiler_params=_params("arbitrary"),
        name="moe_dispatch",
    )(tile, ex, chunk, flags, pos_t, xb)


def _experts_kernel(texp_ref, nact_ref, xs_ref, wg_ref, wu_ref, wd_ref, ys_ref, acc_ref, *, nf):
    j = pl.program_id(0)
    f = pl.program_id(1)

    @pl.when(j < nact_ref[0])
    def _():
        @pl.when(f == 0)
        def _():
            acc_ref[...] = jnp.zeros(acc_ref.shape, F32)

        acc_ref[...] += _swiglu_chunk(xs_ref[...], wg_ref[...], wu_ref[...], wd_ref[...])

        @pl.when(f == nf - 1)
        def _():
            ys_ref[...] = acc_ref[...].astype(BF16)


def _experts(layer_moe, xs, tile_expert, n_active, wg, wu, wd, tm):
    nt = xs.shape[0] // tm
    nf = 2
    tf = wg.shape[3] // nf

    def row_map(j, f, te, na):
        return (jnp.minimum(j, na[0] - 1), 0)

    def col_f(j, f, na):
        return jnp.where(j < na[0], f, nf - 1)

    def expert(j, te, na):
        return te[jnp.minimum(j, na[0] - 1)]

    return pl.pallas_call(
        functools.partial(_experts_kernel, nf=nf),
        grid_spec=pltpu.PrefetchScalarGridSpec(
            num_scalar_prefetch=2,
            grid=(nt, nf),
            in_specs=[pl.BlockSpec((tm, D_MODEL), row_map),
                      pl.BlockSpec((None, None, D_MODEL, tf),
                                   lambda j, f, te, na: (layer_moe, expert(j, te, na), 0, col_f(j, f, na))),
                      pl.BlockSpec((None, None, D_MODEL, tf),
                                   lambda j, f, te, na: (layer_moe, expert(j, te, na), 0, col_f(j, f, na))),
                      pl.BlockSpec((None, None, tf, D_MODEL),
                                   lambda j, f, te, na: (layer_moe, expert(j, te, na), col_f(j, f, na), 0))],
            out_specs=pl.BlockSpec((tm, D_MODEL), row_map),
            scratch_shapes=[pltpu.VMEM((tm, D_MODEL), F32)]),
        out_shape=jax.ShapeDtypeStruct(xs.shape, BF16),
        compiler_params=_params("arbitrary", "arbitrary"),
        name="moe_experts",
    )(tile_expert, n_active, xs, wg, wu, wd)


def _combine_kernel(ti_ref, ex_ref, chunk_ref, flag_ref, x_ref, pos_ref, comb_ref, ys_ref, lg_ref, lb_ref,
                    x2_ref, x2b_ref, acc_ref, *, alpha):
    s = pl.program_id(0)
    flags = flag_ref[s]

    @pl.when((flags & FLAG_ACTIVE) != 0)
    def _():
        tt, tc = acc_ref.shape[0], ys_ref.shape[0]
        lane = lax.broadcasted_iota(jnp.int32, (tt, LANES), 1)
        mine = lane == ex_ref[s]
        row = jnp.max(jnp.where(mine, pos_ref[...], -1.0), axis=-1, keepdims=True)
        gate = jnp.sum(jnp.where(mine, comb_ref[...], 0.0), axis=-1, keepdims=True)
        col = (lax.broadcasted_iota(jnp.int32, (tt, tc), 1) + chunk_ref[s] * tc).astype(F32)
        pick = jnp.where(row == col, 1.0, 0.0).astype(BF16)
        _accumulate(flags, acc_ref, gate * _dot(pick, ys_ref[...]))

        @pl.when((flags & FLAG_LAST) != 0)
        def _():
            x2 = _layer_norm(alpha * x_ref[...] + acc_ref[...], lg_ref[...], lb_ref[...])
            x2_ref[...] = x2
            x2b_ref[...] = x2.astype(BF16)


def _combine(x, pos, comb, ys, plan, ln_g, ln_b, alpha, tm):
    ti, ex, chunk, flags = plan
    m = x.shape[0]
    pos_pad = jnp.pad(pos, ((0, 0), (0, LANES - N_EXPERTS)), constant_values=-1.0)

    def tok(wd):
        return pl.BlockSpec((tm, wd), lambda s, ti, ex, c, f: (ti[s], 0))

    vec = pl.BlockSpec((1, D_MODEL), lambda s, ti, ex, c, f: (0, 0))
    return pl.pallas_call(
        functools.partial(_combine_kernel, alpha=alpha),
        grid_spec=pltpu.PrefetchScalarGridSpec(
            num_scalar_prefetch=4,
            grid=(ti.shape[0],),
            in_specs=[tok(D_MODEL), tok(LANES), tok(LANES),
                      pl.BlockSpec((tm, D_MODEL), lambda s, ti, ex, c, f: (c[s], 0)), vec, vec],
            out_specs=[tok(D_MODEL), tok(D_MODEL)],
            scratch_shapes=[pltpu.VMEM((tm, D_MODEL), F32)]),
        out_shape=[jax.ShapeDtypeStruct((m, D_MODEL), F32), jax.ShapeDtypeStruct((m, D_MODEL), BF16)],
        compiler_params=_params("arbitrary"),
        name="moe_combine",
    )(ti, ex, chunk, flags, x, pos_pad, comb, ys, ln_g, ln_b)


def _moe(layer_moe, x, xb, comb, wg, wu, wd, ln_g, ln_b, alpha):
    tm = MOE_TILE
    assert x.shape[0] % tm == 0
    pos, tile_expert, n_active, dispatch_plan, combine_plan = _route_plan(comb, tm)
    xs = _dispatch(xb, pos.T, dispatch_plan, tile_expert.shape[0], tm)
    ys = _experts(layer_moe, xs, tile_expert, n_active, wg, wu, wd, tm)
    return _combine(x, pos, comb, ys, combine_plan, ln_g, ln_b, alpha, tm)


def _seq_minor(cache):
    d, n, length, h, hd = cache.shape
    return jnp.transpose(cache, (0, 1, 3, 4, 2)).reshape(d, n, h * hd, length)


def _seq_major(x, heads):
    d, n, hw, length = x.shape
    return jnp.transpose(x.reshape(d, n, heads, hw // heads, length), (0, 1, 4, 2, 3))


def kernel(x_prompt, x_sample, cache_a_k, cache_a_v, cache_b_k, cache_b_v, w_in, attn_sink, w_branch_a,
           w_branch_b, w_out, ln1_g, ln1_b, ln2_g, ln2_b, ffn_w_gate, ffn_w_up, ffn_w_down, moe_router,
           moe_w_gate, moe_w_up, moe_w_down):
    depth = w_in.shape[0]
    bp, seq, _ = x_prompt.shape
    nb, nq, _ = x_sample.shape
    la, lb = cache_a_k.shape[2], cache_b_k.shape[2]
    assert seq % 512 == 0 and seq <= W_MAX and la == WIN_A and lb == W_MAX and nq == 8
    mp, ms = bp * seq, nb * nq
    alpha = (2.0 * depth) ** 0.25

    n_heads = H_A + H_B
    slopes = jnp.exp2(-8.0 * jnp.arange(1, n_heads + 1, dtype=F32) / n_heads)
    slopes_a, slopes_b = slopes[0::2], slopes[1::2]
    tabs = (_tile_bias(slopes_a, BLK, 2 * BLK, BLK, 1, WIN_A - 1),
            _tile_bias(slopes_b, BLK, 2 * BLK, BLK, DILATED_PAIRS[0][1], DILATED_PAIRS[0][0] // DILATED_PAIRS[0][1]),
            _tile_bias(slopes_b, BLK, 2 * BLK, BLK, DILATED_PAIRS[1][1], DILATED_PAIRS[1][0] // DILATED_PAIRS[1][1]),
            _tile_bias(slopes_b, seq // 16, seq // 16, 0, DILATED_PAIRS[2][1],
                       DILATED_PAIRS[2][0] // DILATED_PAIRS[2][1]))
    biases = _sample_bias(slopes_a, nq, la, WIN_A - 1, False) + _sample_bias(slopes_b, nq, lb, W_MAX, True)

    cak, cav, cbk, cbv = (_seq_minor(c) for c in (cache_a_k, cache_a_v, cache_b_k, cache_b_v))

    w_in_b = w_in.astype(BF16)
    kv0, kv1 = QA_W, QA_W + KVA_W
    kb0 = kv1 + HB_W
    w_in_t = jnp.transpose(jnp.concatenate([w_in_b[:, :, kv0:kv1], w_in_b[:, :, kb0:kb0 + 2 * HB_W]], axis=2),
                           (0, 2, 1))
    wa_b, wb_b, wo_b = w_branch_a.astype(BF16), w_branch_b.astype(BF16), w_out.astype(BF16)
    fg_b, fu_b, fd_b = ffn_w_gate.astype(BF16), ffn_w_up.astype(BF16), ffn_w_down.astype(BF16)
    mg_b, mu_b, md_b = moe_w_gate.astype(BF16), moe_w_up.astype(BF16), moe_w_down.astype(BF16)
    router_pad = jnp.pad(moe_router.astype(F32), ((0, 0), (0, 0), (0, LANES - N_EXPERTS)))

    x = jnp.concatenate([x_prompt.reshape(mp, D_MODEL), x_sample.reshape(ms, D_MODEL)], axis=0)
    xb = x.astype(BF16)

    pak, pav, pbk, pbv, sak, sav = [], [], [], [], [], []
    chain = None
    for l in range(depth):
        qa, kva, qb, kb, vb, g, kat, vat, kbt, vbt = _inproj(xb, w_in_b[l], w_in_t[l], bp, seq)
        oa_p, ob_p = _attn_prompt(attn_sink[l], qa, qb, kva, kb, vb, tabs, bp, seq)
        oa_s, ob_s, nak, nav, nbk, nbv = _attn_sample(l, depth, attn_sink[l], qa, qb, kva, kb, vb,
                                                      cak, cav, cbk, cbv, biases, mp, nb, nq, chain)
        chain = (nbk, nbv)
        oa = jnp.concatenate([oa_p, oa_s.astype(BF16)], axis=0)
        ob = jnp.concatenate([ob_p, ob_s.astype(BF16)], axis=0)
        pak.append(kat[:, :, seq - WIN_A:])
        pav.append(vat[:, :, seq - WIN_A:])
        pbk.append(kbt)
        pbv.append(vbt)
        sak.append(nak)
        sav.append(nav)

        ln1 = (ln1_g[l].reshape(1, D_MODEL), ln1_b[l].reshape(1, D_MODEL))
        ln2 = (ln2_g[l].reshape(1, D_MODEL), ln2_b[l].reshape(1, D_MODEL))
        i = l // 2
        if l % 2 == 0:
            x1, x1b = _post(x, oa, ob, g, wa_b[l], wb_b[l], wo_b[l], *ln1, alpha)
            x, xb = _ffn(x1, x1b, fg_b[i], fu_b[i], fd_b[i], *ln2, alpha)
        else:
            x1, x1b, comb = _post(x, oa, ob, g, wa_b[l], wb_b[l], wo_b[l], *ln1, alpha, router_pad[i])
            x, xb = _moe(i, x1, x1b, comb, mg_b, mu_b, md_b, *ln2, alpha)

    return (x[:mp].reshape(bp, seq, D_MODEL), x[mp:].reshape(nb, nq, D_MODEL),
            _seq_major(jnp.stack(pak), KV_A), _seq_major(jnp.stack(pav), KV_A),
            _seq_major(jnp.stack(pbk), H_B), _seq_major(jnp.stack(pbv), H_B),
            _seq_major(jnp.stack(sak), KV_A), _seq_major(jnp.stack(sav), KV_A),
            _seq_major(chain[0], H_B), _seq_major(chain[1], H_B))
```

```python
import functools
import math

import jax
import jax.numpy as jnp
from jax import lax
from jax.experimental import pallas as pl
from jax.experimental.pallas import tpu as pltpu

F32 = jnp.float32
BF16 = jnp.bfloat16

D_MODEL = 1024
HEAD_DIM = 64
H_A = 8
KV_A = 2
WIN_A = 128
H_B = 8
DILATED_PAIRS = ((128, 1), (512, 4), (2048, 16))
W_MAX = 2048
N_EXPERTS = 8
LN_EPS = 1e-5
NEG_INF = -1e30
QA_W = H_A * HEAD_DIM
KVA_W = 2 * KV_A * HEAD_DIM
HB_W = H_B * HEAD_DIM
IN_COLS = QA_W + KVA_W + 3 * HB_W + 2 * D_MODEL
LANES = 128
VMEM_LIMIT_BYTES = 56 * 1024 * 1024


def _params(*sem):
    return pltpu.CompilerParams(dimension_semantics=sem, vmem_limit_bytes=VMEM_LIMIT_BYTES)


def _row_tile(m, cap):
    t = cap
    while m % t:
        t //= 2
    assert t >= 8, (m, cap)
    return t


def _sigmoid(x):
    return 1.0 / (1.0 + jnp.exp(-x))


def _layer_norm(z, g, b):
    mu = jnp.mean(z, axis=-1, keepdims=True)
    zc = z - mu
    var = jnp.mean(zc * zc, axis=-1, keepdims=True)
    return zc * lax.rsqrt(var + LN_EPS) * g + b


def _dot(a, b):
    return jnp.dot(a, b, preferred_element_type=F32)


def _dot_nt(a, b):
    return lax.dot_general(a, b, (((1,), (1,)), ((), ())), preferred_element_type=F32)


KVT_ROWS = 2 * KV_A * HEAD_DIM + 2 * HB_W


def _inproj_kernel(*refs, prompt_tiles, chained):
    x_ref, w_ref, wt_ref = refs[:3]
    (qa_ref, kva_ref, qb_ref, kb_ref, vb_ref, g_ref,
     kat_ref, vat_ref, kbt_ref, vbt_ref) = refs[3 + (2 if chained else 0):]
    x = x_ref[...]
    scale = HEAD_DIM ** -0.5

    def mm(c0, c1):
        return _dot(x, w_ref[:, c0:c1])

    c = 0
    qa_ref[...] = mm(c, c + QA_W) * scale
    c += QA_W
    kva_ref[...] = mm(c, c + KVA_W)
    c += KVA_W
    qb_ref[...] = mm(c, c + HB_W) * scale
    c += HB_W
    kb_ref[...] = mm(c, c + HB_W)
    c += HB_W
    vb_ref[...] = mm(c, c + HB_W)
    c += HB_W
    g_ref[...] = mm(c, c + 2 * D_MODEL)

    @pl.when(pl.program_id(0) < prompt_tiles)
    def _seq_minor_outputs():
        def mt(r0, r1):
            return _dot_nt(wt_ref[r0:r1, :], x)
        ga = KV_A * HEAD_DIM
        kat_ref[...] = mt(0, ga)
        vat_ref[...] = mt(ga, 2 * ga)
        kbt_ref[...] = mt(2 * ga, 2 * ga + HB_W)
        vbt_ref[...] = mt(2 * ga + HB_W, 2 * ga + 2 * HB_W)


def _inproj(xb, w, wt, bp, seq, layer, depth, chain):
    m = xb.shape[0]
    tm = _row_tile(math.gcd(m, seq), 512)
    tiles_per_seq = seq // tm
    prompt_tiles = bp * tiles_per_seq
    widths = (QA_W, KVA_W, HB_W, HB_W, HB_W, 2 * D_MODEL)
    ga = KV_A * HEAD_DIM

    def seq_tile(i):
        j = jnp.minimum(i, prompt_tiles - 1)
        return j // tiles_per_seq, j % tiles_per_seq

    def per_layer(rows):
        def index(i):
            b, j = seq_tile(i)
            return (b, 0, j)
        return pl.BlockSpec((None, rows, tm), index)

    def stacked(rows):
        def index(i):
            b, j = seq_tile(i)
            return (layer, b, 0, j)
        return pl.BlockSpec((None, None, rows, tm), index)

    in_specs = [pl.BlockSpec((tm, D_MODEL), lambda i: (i, 0)),
                pl.BlockSpec((D_MODEL, IN_COLS), lambda i: (0, 0)),
                pl.BlockSpec((KVT_ROWS, D_MODEL), lambda i: (0, 0))]
    args = [xb, w, wt]
    aliases = {}
    if chain is not None:
        in_specs += [pl.BlockSpec(memory_space=pl.ANY)] * 2
        aliases = {3: 8, 4: 9}
        args += list(chain)
    return pl.pallas_call(
        functools.partial(_inproj_kernel, prompt_tiles=prompt_tiles, chained=chain is not None),
        grid=(m // tm,),
        in_specs=in_specs,
        out_specs=[pl.BlockSpec((tm, wd), lambda i: (i, 0)) for wd in widths]
        + [per_layer(ga), per_layer(ga), stacked(HB_W), stacked(HB_W)],
        out_shape=[jax.ShapeDtypeStruct((m, wd), F32) for wd in widths]
        + [jax.ShapeDtypeStruct((bp, ga, seq), F32)] * 2 + [jax.ShapeDtypeStruct((depth, bp, HB_W, seq), F32)] * 2,
        input_output_aliases=aliases,
        compiler_params=_params("arbitrary"),
        name="inproj",
    )(*args)


BLK = 128
HEADS_PER_STEP = 2


def _tile_bias(slopes, n_q, n_k, shift, dist_scale, max_sub):
    r = jnp.arange(n_q, dtype=jnp.int32)[:, None]
    c = jnp.arange(n_k, dtype=jnp.int32)[None, :]
    dsub = shift + r - c
    ok = (dsub >= 0) & (dsub <= max_sub)
    b = -slopes[:, None, None] * (dist_scale * dsub).astype(F32)[None]
    full = jnp.where(ok[None], b, NEG_INF)
    first = jnp.where((ok & (c >= shift))[None], b, NEG_INF)
    return jnp.stack([full, first], axis=1).astype(F32)


def _pair_tile(q_s, rows, k, v, bias_ref, first, sinks=None):
    low = lax.broadcasted_iota(jnp.int32, (1, HEADS_PER_STEP * HEAD_DIM), 1) < HEAD_DIM
    pv, inv, lse = [], [], []
    for hh in range(HEADS_PER_STEP):
        s = _dot_nt(q_s[hh, rows, :], k) + bias_ref[hh, first]
        m = jnp.max(s, axis=-1, keepdims=True)
        if sinks is not None:
            m = jnp.maximum(m, sinks[hh])
        p = jnp.exp(s - m)
        l = jnp.sum(p, axis=-1, keepdims=True)
        if sinks is not None:
            l = l + jnp.exp(sinks[hh] - m)
        pv.append(_dot(p.astype(BF16), v))
        inv.append(1.0 / l)
        lse.append(m + jnp.log(l))
    return jnp.where(low, pv[0], pv[1]) * jnp.where(low, inv[0], inv[1]), jnp.where(low, lse[0], lse[1])


def _attn_prompt_kernel(sink_ref, qa_ref, qb_ref, kva_ref, kb_ref, vb_ref, ta_ref, t1_ref, t2_ref, t3_ref,
                        oa_ref, ob_ref,
                        qa_s, ka_s, va_s, q1_s, k1_s, v1_s, q4_s, k4_s, v4_s, q16_s, k16_s, v16_s, ro_s, rl_s,
                        *, seq):
    hp = pl.program_id(1)
    nblk = seq // BLK
    n4, n16 = seq // 4, seq // 16
    sub4 = n4 + BLK
    hd = HEAD_DIM
    low = lax.broadcasted_iota(jnp.int32, (1, HEADS_PER_STEP * hd), 1) < hd

    @pl.when((pl.program_id(0) == 0) & (hp == 0))
    def _zero_key_pads():
        for ref in (ka_s, va_s, k1_s, v1_s, k4_s, v4_s):
            ref[...] = jnp.zeros(ref.shape, BF16)

    def put_queries(q_s, dst, q):
        q_s[0, dst, :] = jnp.where(low, q, 0.0).astype(BF16)
        q_s[1, dst, :] = jnp.where(low, 0.0, q).astype(BF16)

    group0 = (hp // (H_A // KV_A // HEADS_PER_STEP)) == 0

    def both_halves(x):
        return jnp.where(low == group0, x, pltpu.roll(x, hd, axis=1)).astype(BF16)

    def stage_natural(c, carry):
        r0 = pl.multiple_of(c * BLK, BLK)
        rows, prows = pl.ds(r0, BLK), pl.ds(r0 + BLK, BLK)
        kva = kva_ref[rows, :]
        ka_s[prows, :] = both_halves(kva[:, 0:2 * hd])
        va_s[prows, :] = both_halves(kva[:, 2 * hd:4 * hd])
        put_queries(qa_s, rows, qa_ref[rows, :])
        put_queries(q1_s, rows, qb_ref[rows, :])
        k1_s[prows, :] = kb_ref[rows, :].astype(BF16)
        v1_s[prows, :] = vb_ref[rows, :].astype(BF16)
        return carry
    lax.fori_loop(0, nblk, stage_natural, 0, unroll=4)

    for r4 in range(4):
        def stage_mod4(c, carry, r4=r4):
            j0 = pl.multiple_of(c * BLK, BLK)
            src = pl.ds(r4 + 4 * j0, BLK, stride=4)
            put_queries(q4_s, pl.ds(r4 * n4 + j0, BLK), qb_ref[src, :])
            k4_s[pl.ds(r4 * sub4 + BLK + j0, BLK), :] = kb_ref[src, :].astype(BF16)
            v4_s[pl.ds(r4 * sub4 + BLK + j0, BLK), :] = vb_ref[src, :].astype(BF16)
            return carry
        lax.fori_loop(0, n4 // BLK, stage_mod4, 0, unroll=True)

    def stage_mod16(r, carry):
        src = pl.ds(r, n16, stride=16)
        dst = pl.ds(pl.multiple_of(r * n16, n16), n16)
        put_queries(q16_s, dst, qb_ref[src, :])
        k16_s[dst, :] = kb_ref[src, :].astype(BF16)
        v16_s[dst, :] = vb_ref[src, :].astype(BF16)
        return carry
    lax.fori_loop(0, 16, stage_mod16, 0, unroll=4)

    sinks = [sink_ref[HEADS_PER_STEP * hp + hh] for hh in range(HEADS_PER_STEP)]

    def mixer_a(i, carry):
        r0 = pl.multiple_of(i * BLK, BLK)
        o, _ = _pair_tile(qa_s, pl.ds(r0, BLK), ka_s[pl.ds(r0, 2 * BLK), :], va_s[pl.ds(r0, 2 * BLK), :],
                          ta_ref, jnp.where(i == 0, 1, 0), sinks)
        oa_ref[pl.ds(r0, BLK), :] = o.astype(BF16)
        return carry
    lax.fori_loop(0, nblk, mixer_a, 0, unroll=True)

    def keep(pair, dst, o_lse):
        ro_s[pair, dst, :] = o_lse[0]
        rl_s[pair, dst, :] = jnp.broadcast_to(o_lse[1], o_lse[0].shape)

    def pair_w128(i, carry):
        r0 = pl.multiple_of(i * BLK, BLK)
        keep(0, pl.ds(r0, BLK), _pair_tile(q1_s, pl.ds(r0, BLK), k1_s[pl.ds(r0, 2 * BLK), :],
                                          v1_s[pl.ds(r0, 2 * BLK), :], t1_ref, jnp.where(i == 0, 1, 0)))
        return carry
    lax.fori_loop(0, nblk, pair_w128, 0, unroll=True)

    for r4 in range(4):
        def pair_w512(i, carry, r4=r4):
            j0 = pl.multiple_of(i * BLK, BLK)
            keep(1, pl.ds(r4 + 4 * j0, BLK, stride=4),
                 _pair_tile(q4_s, pl.ds(r4 * n4 + j0, BLK), k4_s[pl.ds(r4 * sub4 + j0, 2 * BLK), :],
                            v4_s[pl.ds(r4 * sub4 + j0, 2 * BLK), :], t2_ref, jnp.where(i == 0, 1, 0)))
            return carry
        lax.fori_loop(0, n4 // BLK, pair_w512, 0, unroll=True)

    def pair_w2048(r, carry):
        src = pl.ds(pl.multiple_of(r * n16, n16), n16)
        keep(2, pl.ds(r, n16, stride=16), _pair_tile(q16_s, src, k16_s[src, :], v16_s[src, :], t3_ref, 0))
        return carry
    lax.fori_loop(0, 16, pair_w2048, 0, unroll=True)

    def merge(i, carry):
        rows = pl.ds(pl.multiple_of(i * BLK, BLK), BLK)
        l0, l1, l2 = rl_s[0, rows, :], rl_s[1, rows, :], rl_s[2, rows, :]
        mx = jnp.maximum(l0, jnp.maximum(l1, l2))
        w0, w1, w2 = jnp.exp(l0 - mx), jnp.exp(l1 - mx), jnp.exp(l2 - mx)
        o = (w0 * ro_s[0, rows, :] + w1 * ro_s[1, rows, :] + w2 * ro_s[2, rows, :]) / (w0 + w1 + w2)
        ob_ref[rows, :] = o.astype(BF16)
        return carry
    lax.fori_loop(0, nblk, merge, 0, unroll=4)


def _attn_prompt(sink, qa, qb, kva, kb, vb, tabs, batch, seq):
    mp = batch * seq
    n4 = seq // 4
    lanes = HEADS_PER_STEP * HEAD_DIM
    smem = pl.BlockSpec(memory_space=pltpu.SMEM)
    cols = pl.BlockSpec((seq, lanes), lambda b, hp: (b, hp))

    def tab(t):
        return pl.BlockSpec((HEADS_PER_STEP,) + t.shape[1:], lambda b, hp: (hp, 0, 0, 0))

    def masked_q():
        return pltpu.VMEM((HEADS_PER_STEP, seq, lanes), BF16)

    def packed(rows):
        return pltpu.VMEM((rows, lanes), BF16)

    return pl.pallas_call(
        functools.partial(_attn_prompt_kernel, seq=seq),
        grid=(batch, H_B // HEADS_PER_STEP),
        in_specs=[smem, cols, cols, pl.BlockSpec((seq, KVA_W), lambda b, hp: (b, 0)), cols, cols]
        + [tab(t) for t in tabs],
        out_specs=[cols, cols],
        out_shape=[jax.ShapeDtypeStruct((mp, QA_W), BF16), jax.ShapeDtypeStruct((mp, HB_W), BF16)],
        scratch_shapes=[masked_q(), packed(seq + BLK), packed(seq + BLK),
                        masked_q(), packed(seq + BLK), packed(seq + BLK),
                        masked_q(), packed(4 * (n4 + BLK)), packed(4 * (n4 + BLK)),
                        masked_q(), packed(seq), packed(seq),
                        pltpu.VMEM((3, seq, lanes), F32), pltpu.VMEM((3, seq, lanes), F32)],
        compiler_params=_params("arbitrary", "arbitrary"),
        name="attn_prompt",
    )(sink, qa, qb, kva, kb, vb, *tabs)


def _sample_bias(slopes, nq, cache_len, max_dist, dilated):
    i = jnp.arange(nq, dtype=jnp.int32)[:, None]

    def make(dist, real):
        if dilated:
            cnt = jnp.zeros(dist.shape, jnp.int32)
            for w, d in DILATED_PAIRS:
                cnt = cnt + jnp.where(((dist & (d - 1)) == 0) & (dist <= w), 1, 0)
            ok = real & (dist >= 0) & (cnt > 0)
            extra = jnp.where(cnt == 3, math.log(3.0), jnp.where(cnt == 2, math.log(2.0), 0.0)).astype(F32)
        else:
            ok = real & (dist >= 0) & (dist <= max_dist)
            extra = jnp.zeros(dist.shape, F32)
        bias = -slopes[:, None, None] * dist.astype(F32)[None] + extra[None]
        return jnp.where(ok[None], bias, NEG_INF).reshape(slopes.shape[0] * nq, dist.shape[1]).astype(F32)

    key = jnp.arange(cache_len, dtype=jnp.int32)[None, :]
    new = jnp.arange(LANES, dtype=jnp.int32)[None, :] - (LANES - nq)
    return make(cache_len + i - key, key >= 0), make(i - new, new >= 0)


def _attn_sample_kernel(*refs, nq, la, lb, chained):
    (sink_ref, qa_ref, qb_ref, kva_ref, kb_ref, vb_ref, cak_ref, cav_ref, cbk_ref, cbv_ref,
     bac_ref, ban_ref, bbc_ref, bbn_ref) = refs[:14]
    refs = refs[14 + (2 if chained else 0):]
    oa_ref, ob_ref, nak_ref, nav_ref, nbk_ref, nbv_ref, s_s, p_s = refs
    rows = H_A * nq
    hd = HEAD_DIM
    ga = KV_A * hd
    rep = H_A // KV_A

    def new_rows(x):
        return jnp.concatenate([jnp.zeros((LANES - nq, x.shape[1]), F32), x], axis=0)

    lane = lax.broadcasted_iota(jnp.int32, (1, LANES), 1)

    def shifted(cache_ref, out_ref, new_t, length, row_chunk):
        for r0 in range(0, cache_ref.shape[0], row_chunk):
            rs = slice(r0, r0 + row_chunk)
            rolled = pltpu.roll(cache_ref[rs, :], length - nq, axis=1)
            if length > LANES:
                out_ref[rs, 0:length - LANES] = rolled[:, 0:length - LANES]
            out_ref[rs, length - LANES:length] = jnp.where(lane < LANES - nq, rolled[:, length - LANES:length],
                                                           new_t[rs, :])

    qa = qa_ref[...]
    kva = kva_ref[...]
    zeros = jnp.zeros((nq, hd), F32)
    qbd_a = jnp.concatenate(
        [jnp.concatenate([qa[:, h * hd:(h + 1) * hd] if g == h // rep else zeros for g in range(KV_A)], axis=1)
         for h in range(H_A)], axis=0).astype(BF16)
    kn_a = new_rows(kva[:, 0:ga])
    vn_a = new_rows(kva[:, ga:2 * ga])
    hrow = lax.broadcasted_iota(jnp.int32, (rows, 1), 0) // nq
    sink = jnp.zeros((rows, 1), F32)
    for h in range(H_A):
        sink = jnp.where(hrow == h, sink_ref[h], sink)
    s_c = _dot(qbd_a, cak_ref[...].astype(BF16)) + bac_ref[...]
    s_n = _dot_nt(qbd_a, kn_a.astype(BF16)) + ban_ref[...]
    m = jnp.maximum(jnp.maximum(jnp.max(s_c, axis=-1, keepdims=True), jnp.max(s_n, axis=-1, keepdims=True)), sink)
    p_c, p_n = jnp.exp(s_c - m), jnp.exp(s_n - m)
    l = jnp.sum(p_c, axis=-1, keepdims=True) + jnp.sum(p_n, axis=-1, keepdims=True) + jnp.exp(sink - m)
    o = (_dot_nt(p_c.astype(BF16), cav_ref[...].astype(BF16)) + _dot(p_n.astype(BF16), vn_a.astype(BF16))) / l
    oa_ref[...] = jnp.concatenate(
        [o[h * nq:(h + 1) * nq, (h // rep) * hd:(h // rep + 1) * hd] for h in range(H_A)], axis=1)
    shifted(cak_ref, nak_ref, kn_a.T, la, ga)
    shifted(cav_ref, nav_ref, vn_a.T, la, ga)

    r = lax.broadcasted_iota(jnp.int32, (rows, HB_W), 0)
    c = lax.broadcasted_iota(jnp.int32, (rows, HB_W), 1)
    own = (r // nq) == (c // hd)
    qbd = jnp.where(own, jnp.concatenate([qb_ref[...]] * H_B, axis=0), 0.0).astype(BF16)
    kn_b = new_rows(kb_ref[...])
    vn_b = new_rows(vb_ref[...])
    chunk = 512
    for c0 in range(0, lb, chunk):
        cs = slice(c0, c0 + chunk)
        s_s[:, cs] = _dot(qbd, cbk_ref[:, cs].astype(BF16)) + bbc_ref[:, cs]
    s_n = _dot_nt(qbd, kn_b.astype(BF16)) + bbn_ref[...]
    s_c = s_s[...]
    m = jnp.maximum(jnp.max(s_c, axis=-1, keepdims=True), jnp.max(s_n, axis=-1, keepdims=True))
    p_c, p_n = jnp.exp(s_c - m), jnp.exp(s_n - m)
    l = jnp.sum(p_c, axis=-1, keepdims=True) + jnp.sum(p_n, axis=-1, keepdims=True)
    p_s[...] = p_c.astype(BF16)
    o = _dot(p_n.astype(BF16), vn_b.astype(BF16))
    for c0 in range(0, lb, chunk):
        cs = slice(c0, c0 + chunk)
        o = o + _dot_nt(p_s[:, cs], cbv_ref[:, cs].astype(BF16))
    o = jnp.where(own, o, 0.0) / l
    out = o[0:nq]
    for h in range(1, H_B):
        out = out + o[h * nq:(h + 1) * nq]
    ob_ref[...] = out
    shifted(cbk_ref, nbk_ref, kn_b.T, lb, 64)
    shifted(cbv_ref, nbv_ref, vn_b.T, lb, 64)


def _attn_sample(layer, depth, sink, qa, qb, kva, kb, vb, cak, cav, cbk, cbv, biases, row0, nb, nq, chain):
    la, lb = cak.shape[3], cbk.shape[3]
    ga = KV_A * HEAD_DIM
    rows = H_A * nq
    blk0 = row0 // nq
    smem = pl.BlockSpec(memory_space=pltpu.SMEM)

    def tok(wd):
        return pl.BlockSpec((nq, wd), lambda b: (blk0 + b, 0))

    def cache(r, length):
        return pl.BlockSpec((None, None, r, length), lambda b: (layer, b, 0, 0))

    def const(a):
        return pl.BlockSpec(a.shape, lambda b: (0, 0))

    in_specs = [smem, tok(QA_W), tok(HB_W), tok(KVA_W), tok(HB_W), tok(HB_W),
                cache(ga, la), cache(ga, la), cache(HB_W, lb), cache(HB_W, lb)] + [const(a) for a in biases]
    args = [sink, qa, qb, kva, kb, vb, cak, cav, cbk, cbv, *biases]
    aliases = {}
    if chain is not None:
        in_specs += [pl.BlockSpec(memory_space=pl.ANY)] * 2
        aliases = {len(args): 4, len(args) + 1: 5}
        args += list(chain)
    out_tok = pl.BlockSpec((nq, QA_W), lambda b: (b, 0))
    return pl.pallas_call(
        functools.partial(_attn_sample_kernel, nq=nq, la=la, lb=lb, chained=chain is not None),
        grid=(nb,),
        in_specs=in_specs,
        out_specs=[out_tok, out_tok,
                   pl.BlockSpec((None, ga, la), lambda b: (b, 0, 0)), pl.BlockSpec((None, ga, la), lambda b: (b, 0, 0)),
                   cache(HB_W, lb), cache(HB_W, lb)],
        out_shape=[jax.ShapeDtypeStruct((nb * nq, QA_W), F32), jax.ShapeDtypeStruct((nb * nq, HB_W), F32),
                   jax.ShapeDtypeStruct((nb, ga, la), F32), jax.ShapeDtypeStruct((nb, ga, la), F32),
                   jax.ShapeDtypeStruct((depth, nb, HB_W, lb), F32), jax.ShapeDtypeStruct((depth, nb, HB_W, lb), F32)],
        scratch_shapes=[pltpu.VMEM((rows, lb), F32), pltpu.VMEM((rows, lb), BF16)],
        input_output_aliases=aliases,
        compiler_params=_params("arbitrary"),
        name="attn_sample",
    )(*args)


def _post_kernel(*refs, alpha, with_router, prompt_tiles):
    if with_router:
        (x_ref, oap_ref, obp_ref, oas_ref, obs_ref, g_ref, wa_ref, wb_ref, wo_ref, lg_ref, lb_ref, wrh_ref, wrl_ref,
         x1_ref, x1b_ref, comb_ref) = refs
    else:
        (x_ref, oap_ref, obp_ref, oas_ref, obs_ref, g_ref, wa_ref, wb_ref, wo_ref, lg_ref, lb_ref,
         x1_ref, x1b_ref) = refs
    prompt = pl.program_id(0) < prompt_tiles
    oa = jnp.where(prompt, oap_ref[...], oas_ref[...].astype(BF16))
    ob = jnp.where(prompt, obp_ref[...], obs_ref[...].astype(BF16))
    a = _dot(oa, wa_ref[...])
    b = _dot(ob, wb_ref[...])
    g = g_ref[...]
    merged = _sigmoid(g[:, :D_MODEL]) * a + _sigmoid(g[:, D_MODEL:]) * b
    y = _dot(merged.astype(BF16), wo_ref[...])
    x1 = _layer_norm(alpha * x_ref[...] + y, lg_ref[...], lb_ref[...])
    x1b = x1.astype(BF16)
    x1_ref[...] = x1
    x1b_ref[...] = x1b
    if with_router:
        x1l = (x1 - x1b.astype(F32)).astype(BF16)
        logits = _dot(x1b, wrh_ref[...]) + (_dot(x1l, wrh_ref[...]) + _dot(x1b, wrl_ref[...]))
        lane = lax.broadcasted_iota(jnp.int32, logits.shape, 1)
        logits = jnp.where(lane < N_EXPERTS, logits, -jnp.inf)
        t1 = jnp.max(logits, axis=-1, keepdims=True)
        lane = lane.astype(F32)
        i1 = jnp.min(jnp.where(logits == t1, lane, float(LANES)), axis=-1, keepdims=True)
        rest = jnp.where(lane == i1, -jnp.inf, logits)
        t2 = jnp.max(rest, axis=-1, keepdims=True)
        i2 = jnp.min(jnp.where(rest == t2, lane, float(LANES)), axis=-1, keepdims=True)
        e2 = jnp.exp(t2 - t1)
        den = 1.0 + e2
        comb_ref[...] = jnp.where(lane == i1, 1.0 / den, jnp.where(lane == i2, e2 / den, 0.0))


def _post(x, oa_p, ob_p, oa_s, ob_s, g, wa, wb, wo, ln_g, ln_b, alpha, w_router=None):
    m = x.shape[0]
    tm = _row_tile(math.gcd(oa_p.shape[0], oa_s.shape[0]), 512)
    prompt_tiles = oa_p.shape[0] // tm
    with_router = w_router is not None

    def rows(wd):
        return pl.BlockSpec((tm, wd), lambda i: (i, 0))

    def prompt_rows(wd):
        return pl.BlockSpec((tm, wd), lambda i: (jnp.minimum(i, prompt_tiles - 1), 0))

    def sample_rows(wd):
        return pl.BlockSpec((tm, wd), lambda i: (jnp.maximum(i - prompt_tiles, 0), 0))

    def const(shape):
        return pl.BlockSpec(shape, lambda i: (0, 0))

    in_specs = [rows(D_MODEL), prompt_rows(QA_W), prompt_rows(HB_W), sample_rows(QA_W), sample_rows(HB_W),
                rows(2 * D_MODEL), const(wa.shape), const(wb.shape), const(wo.shape),
                const((1, D_MODEL)), const((1, D_MODEL))]
    args = [x, oa_p, ob_p, oa_s, ob_s, g, wa, wb, wo, ln_g, ln_b]
    out_specs = [rows(D_MODEL), rows(D_MODEL)]
    out_shape = [jax.ShapeDtypeStruct((m, D_MODEL), F32), jax.ShapeDtypeStruct((m, D_MODEL), BF16)]
    if with_router:
        in_specs += [const(w_router[0].shape), const(w_router[1].shape)]
        args += list(w_router)
        out_specs.append(rows(LANES))
        out_shape.append(jax.ShapeDtypeStruct((m, LANES), F32))
    return pl.pallas_call(
        functools.partial(_post_kernel, alpha=alpha, with_router=with_router, prompt_tiles=prompt_tiles),
        grid=(m // tm,),
        in_specs=in_specs, out_specs=out_specs, out_shape=out_shape,
        compiler_params=_params("arbitrary"),
        name="post_router" if with_router else "post",
    )(*args)


def _swiglu_chunk(xb, wg, wu, wd):
    hg = _dot(xb, wg)
    hu = _dot(xb, wu)
    return _dot((hg * _sigmoid(hg) * hu).astype(BF16), wd)


def _ffn_kernel(x_ref, xb_ref, wg_ref, wu_ref, wd_ref, lg_ref, lb_ref, x2_ref, x2b_ref, acc_ref,
                *, alpha, nf):
    f = pl.program_id(1)

    @pl.when(f == 0)
    def _():
        acc_ref[...] = jnp.zeros(acc_ref.shape, F32)

    acc_ref[...] += _swiglu_chunk(xb_ref[...], wg_ref[...], wu_ref[...], wd_ref[...])

    @pl.when(f == nf - 1)
    def _():
        x2 = _layer_norm(alpha * x_ref[...] + acc_ref[...], lg_ref[...], lb_ref[...])
        x2_ref[...] = x2
        x2b_ref[...] = x2.astype(BF16)


def _ffn(x, xb, wg, wu, wd, ln_g, ln_b, alpha):
    m = x.shape[0]
    tm = _row_tile(m, 512)
    d_ff = wg.shape[1]
    nf = 2
    tf = d_ff // nf
    rows = pl.BlockSpec((tm, D_MODEL), lambda i, f: (i, 0))
    vec = pl.BlockSpec((1, D_MODEL), lambda i, f: (0, 0))
    return pl.pallas_call(
        functools.partial(_ffn_kernel, alpha=alpha, nf=nf),
        grid=(m // tm, nf),
        in_specs=[rows, rows,
                  pl.BlockSpec((D_MODEL, tf), lambda i, f: (0, f)),
                  pl.BlockSpec((D_MODEL, tf), lambda i, f: (0, f)),
                  pl.BlockSpec((tf, D_MODEL), lambda i, f: (f, 0)),
                  vec, vec],
        out_specs=[rows, rows],
        out_shape=[jax.ShapeDtypeStruct((m, D_MODEL), F32), jax.ShapeDtypeStruct((m, D_MODEL), BF16)],
        scratch_shapes=[pltpu.VMEM((tm, D_MODEL), F32)],
        compiler_params=_params("arbitrary", "arbitrary"),
        name="ffn",
    )(x, xb, wg, wu, wd, ln_g, ln_b)


MOE_TILE = 512
MOE_SUB = 256
FLAG_FIRST, FLAG_LAST, FLAG_ACTIVE = 1, 2, 4


def _steps(n, lo, total_steps):
    end = jnp.cumsum(n)
    start = end - n
    s = jnp.arange(total_steps, dtype=jnp.int32)
    sc = jnp.minimum(s, end[-1] - 1)
    g = jnp.searchsorted(end, sc, side="right", method="compare_all").astype(jnp.int32)
    return g, (lo[g] + sc - start[g]).astype(jnp.int32), sc, s < end[-1], start, end


def _step_flags(active, first, last):
    return (jnp.where(active & first, FLAG_FIRST, 0) + jnp.where(active & last, FLAG_LAST, 0)
            + jnp.where(active, FLAG_ACTIVE, 0)).astype(jnp.int32)


def _route_plan(comb, tm):
    m = comb.shape[0]
    onehot = comb[:, :N_EXPERTS] > 0.0
    oh = onehot.astype(jnp.int32)
    csum = jnp.cumsum(oh, axis=0)
    cnt = csum[-1]
    tiles_e = (cnt + tm - 1) // tm
    tile_end = jnp.cumsum(tiles_e)
    tile_off = tile_end - tiles_e
    pos = jnp.where(onehot, (tile_off * tm)[None, :] + csum - oh, -1)
    nt = 2 * m // tm + N_EXPERTS
    n_active = tile_end[-1].astype(jnp.int32)
    tiles = jnp.arange(nt, dtype=jnp.int32)
    tile_expert = jnp.minimum(jnp.searchsorted(tile_end, tiles, side="right", method="compare_all"),
                              N_EXPERTS - 1).astype(jnp.int32)

    ts = MOE_SUB
    per = tm // ts
    subs = jnp.arange(nt * per, dtype=jnp.int32)
    sub_expert = tile_expert[subs // per]
    k0 = (subs - tile_off[sub_expert] * per) * ts
    k1 = jnp.minimum(k0 + ts, cnt[sub_expert]) - 1
    find = jax.vmap(lambda col, q: jnp.searchsorted(col, q, side="left", method="compare_all"))

    def token_of_rank(q):
        every = find(csum.T, jnp.broadcast_to(q[None, :], (N_EXPERTS, nt * per)))
        return jnp.take_along_axis(every, sub_expert[None, :], axis=0)[0]

    t_lo, t_hi = token_of_rank(k0 + 1), token_of_rank(k1 + 1)
    has_rows = k1 >= k0
    c_lo = jnp.where(has_rows, t_lo // tm, 0).astype(jnp.int32)
    n = jnp.where(subs < n_active * per, jnp.where(has_rows, t_hi // tm - t_lo // tm + 1, 1), 0).astype(jnp.int32)
    g, chunk, sc, active, start, end = _steps(n, c_lo, nt * per + N_EXPERTS * (m // tm))
    dispatch = (g, sub_expert[g], chunk, _step_flags(active, sc == start[g], sc == end[g] - 1))

    nti = m // tm
    posr = pos.reshape(nti, tm, N_EXPERTS)
    p_hi = jnp.max(posr, axis=1)
    p_lo = jnp.min(jnp.where(posr >= 0, posr, nt * tm), axis=1)
    c_lo = (p_lo // ts).reshape(-1).astype(jnp.int32)
    n = jnp.where(p_hi >= 0, p_hi // ts - p_lo // ts + 1, 0).reshape(-1).astype(jnp.int32)
    g, chunk, sc, active, start, end = _steps(n, c_lo, nti * N_EXPERTS + nt * per)
    ti, ex = g // N_EXPERTS, g % N_EXPERTS
    t_start = start.reshape(nti, N_EXPERTS)[:, 0]
    t_end = end.reshape(nti, N_EXPERTS)[:, -1]
    combine = (ti.astype(jnp.int32), ex.astype(jnp.int32), chunk,
               _step_flags(active, sc == t_start[ti], sc == t_end[ti] - 1))
    return pos.astype(F32), tile_expert, n_active.reshape(1), dispatch, combine


def _accumulate(flags, acc_ref, v):
    @pl.when((flags & FLAG_FIRST) != 0)
    def _():
        acc_ref[...] = v

    @pl.when((flags & FLAG_FIRST) == 0)
    def _():
        acc_ref[...] += v


def _dispatch_kernel(tile_ref, ex_ref, chunk_ref, flag_ref, pos_ref, x_ref, o_ref, acc_ref):
    s = pl.program_id(0)
    flags = flag_ref[s]

    @pl.when((flags & FLAG_ACTIVE) != 0)
    def _():
        ts, tc = acc_ref.shape[0], x_ref.shape[0]
        row = (lax.broadcasted_iota(jnp.int32, (ts, tc), 0) + tile_ref[s] * ts).astype(F32)
        pick = jnp.where(pos_ref[pl.ds(ex_ref[s], 1), :] == row, 1.0, 0.0).astype(BF16)
        _accumulate(flags, acc_ref, _dot(pick, x_ref[...]))

        @pl.when((flags & FLAG_LAST) != 0)
        def _():
            o_ref[...] = acc_ref[...].astype(BF16)


def _dispatch(xb, pos_t, plan, n_tiles, tm):
    tile, ex, chunk, flags = plan
    return pl.pallas_call(
        _dispatch_kernel,
        grid_spec=pltpu.PrefetchScalarGridSpec(
            num_scalar_prefetch=4,
            grid=(tile.shape[0],),
            in_specs=[pl.BlockSpec((N_EXPERTS, tm), lambda s, t, e, c, f: (0, c[s])),
                      pl.BlockSpec((tm, D_MODEL), lambda s, t, e, c, f: (c[s], 0))],
            out_specs=pl.BlockSpec((MOE_SUB, D_MODEL), lambda s, t, e, c, f: (t[s], 0)),
            scratch_shapes=[pltpu.VMEM((MOE_SUB, D_MODEL), F32)]),
        out_shape=jax.ShapeDtypeStruct((n_tiles * tm, D_MODEL), BF16),
        compiler_params=_params("arbitrary"),
        name="moe_dispatch",
    )(tile, ex, chunk, flags, pos_t, xb)


def _experts_kernel(texp_ref, nact_ref, xs_ref, wg_ref, wu_ref, wd_ref, ys_ref, acc_ref, *, nf):
    j = pl.program_id(0)
    f = pl.program_id(1)

    @pl.when(j < nact_ref[0])
    def _():
        @pl.when(f == 0)
        def _():
            acc_ref[...] = jnp.zeros(acc_ref.shape, F32)

        acc_ref[...] += _swiglu_chunk(xs_ref[...], wg_ref[...], wu_ref[...], wd_ref[...])

        @pl.when(f == nf - 1)
        def _():
            ys_ref[...] = acc_ref[...].astype(BF16)


def _experts(layer_moe, xs, tile_expert, n_active, wg, wu, wd, tm):
    nt = xs.shape[0] // tm
    nf = 2
    tf = wg.shape[3] // nf

    def row_map(j, f, te, na):
        return (jnp.minimum(j, na[0] - 1), 0)

    def col_f(j, f, na):
        return jnp.where(j < na[0], f, nf - 1)

    def expert(j, te, na):
        return te[jnp.minimum(j, na[0] - 1)]

    return pl.pallas_call(
        functools.partial(_experts_kernel, nf=nf),
        grid_spec=pltpu.PrefetchScalarGridSpec(
            num_scalar_prefetch=2,
            grid=(nt, nf),
            in_specs=[pl.BlockSpec((tm, D_MODEL), row_map),
                      pl.BlockSpec((None, None, D_MODEL, tf),
                                   lambda j, f, te, na: (layer_moe, expert(j, te, na), 0, col_f(j, f, na))),
                      pl.BlockSpec((None, None, D_MODEL, tf),
                                   lambda j, f, te, na: (layer_moe, expert(j, te, na), 0, col_f(j, f, na))),
                      pl.BlockSpec((None, None, tf, D_MODEL),
                                   lambda j, f, te, na: (layer_moe, expert(j, te, na), col_f(j, f, na), 0))],
            out_specs=pl.BlockSpec((tm, D_MODEL), row_map),
            scratch_shapes=[pltpu.VMEM((tm, D_MODEL), F32)]),
        out_shape=jax.ShapeDtypeStruct(xs.shape, BF16),
        compiler_params=_params("arbitrary", "arbitrary"),
        name="moe_experts",
    )(tile_expert, n_active, xs, wg, wu, wd)


def _combine_kernel(ti_ref, ex_ref, chunk_ref, flag_ref, x_ref, pos_ref, comb_ref, ys_ref, lg_ref, lb_ref,
                    x2_ref, x2b_ref, acc_ref, *, alpha):
    s = pl.program_id(0)
    flags = flag_ref[s]

    @pl.when((flags & FLAG_ACTIVE) != 0)
    def _():
        tt, tc = acc_ref.shape[0], ys_ref.shape[0]
        lane = lax.broadcasted_iota(jnp.int32, (tt, LANES), 1)
        mine = lane == ex_ref[s]
        row = jnp.max(jnp.where(mine, pos_ref[...], -1.0), axis=-1, keepdims=True)
        gate = jnp.sum(jnp.where(mine, comb_ref[...], 0.0), axis=-1, keepdims=True)
        col = (lax.broadcasted_iota(jnp.int32, (tt, tc), 1) + chunk_ref[s] * tc).astype(F32)
        pick = jnp.where(row == col, 1.0, 0.0).astype(BF16)
        _accumulate(flags, acc_ref, gate * _dot(pick, ys_ref[...]))

        @pl.when((flags & FLAG_LAST) != 0)
        def _():
            x2 = _layer_norm(alpha * x_ref[...] + acc_ref[...], lg_ref[...], lb_ref[...])
            x2_ref[...] = x2
            x2b_ref[...] = x2.astype(BF16)


def _combine(x, pos, comb, ys, plan, ln_g, ln_b, alpha, tm):
    ti, ex, chunk, flags = plan
    m = x.shape[0]
    pos_pad = jnp.pad(pos, ((0, 0), (0, LANES - N_EXPERTS)), constant_values=-1.0)

    def tok(wd):
        return pl.BlockSpec((tm, wd), lambda s, ti, ex, c, f: (ti[s], 0))

    vec = pl.BlockSpec((1, D_MODEL), lambda s, ti, ex, c, f: (0, 0))
    return pl.pallas_call(
        functools.partial(_combine_kernel, alpha=alpha),
        grid_spec=pltpu.PrefetchScalarGridSpec(
            num_scalar_prefetch=4,
            grid=(ti.shape[0],),
            in_specs=[tok(D_MODEL), tok(LANES), tok(LANES),
                      pl.BlockSpec((MOE_SUB, D_MODEL), lambda s, ti, ex, c, f: (c[s], 0)), vec, vec],
            out_specs=[tok(D_MODEL), tok(D_MODEL)],
            scratch_shapes=[pltpu.VMEM((tm, D_MODEL), F32)]),
        out_shape=[jax.ShapeDtypeStruct((m, D_MODEL), F32), jax.ShapeDtypeStruct((m, D_MODEL), BF16)],
        compiler_params=_params("arbitrary"),
        name="moe_combine",
    )(ti, ex, chunk, flags, x, pos_pad, comb, ys, ln_g, ln_b)


def _moe(layer_moe, x, xb, comb, wg, wu, wd, ln_g, ln_b, alpha):
    tm = MOE_TILE
    assert x.shape[0] % tm == 0
    pos, tile_expert, n_active, dispatch_plan, combine_plan = _route_plan(comb, tm)
    xs = _dispatch(xb, pos.T, dispatch_plan, tile_expert.shape[0], tm)
    ys = _experts(layer_moe, xs, tile_expert, n_active, wg, wu, wd, tm)
    return _combine(x, pos, comb, ys, combine_plan, ln_g, ln_b, alpha, tm)


def _seq_minor(cache):
    d, n, length, h, hd = cache.shape
    return jnp.transpose(cache, (0, 1, 3, 4, 2)).reshape(d, n, h * hd, length)


def _seq_major(x, heads):
    d, n, hw, length = x.shape
    return jnp.transpose(x.reshape(d, n, heads, hw // heads, length), (0, 1, 4, 2, 3))


def kernel(x_prompt, x_sample, cache_a_k, cache_a_v, cache_b_k, cache_b_v, w_in, attn_sink, w_branch_a,
           w_branch_b, w_out, ln1_g, ln1_b, ln2_g, ln2_b, ffn_w_gate, ffn_w_up, ffn_w_down, moe_router,
           moe_w_gate, moe_w_up, moe_w_down):
    depth = w_in.shape[0]
    bp, seq, _ = x_prompt.shape
    nb, nq, _ = x_sample.shape
    la, lb = cache_a_k.shape[2], cache_b_k.shape[2]
    assert seq % 512 == 0 and seq <= W_MAX and la == WIN_A and lb == W_MAX and nq == 8
    mp, ms = bp * seq, nb * nq
    alpha = (2.0 * depth) ** 0.25

    n_heads = H_A + H_B
    slopes = jnp.exp2(-8.0 * jnp.arange(1, n_heads + 1, dtype=F32) / n_heads)
    slopes_a, slopes_b = slopes[0::2], slopes[1::2]
    tabs = (_tile_bias(slopes_a, BLK, 2 * BLK, BLK, 1, WIN_A - 1),
            _tile_bias(slopes_b, BLK, 2 * BLK, BLK, DILATED_PAIRS[0][1], DILATED_PAIRS[0][0] // DILATED_PAIRS[0][1]),
            _tile_bias(slopes_b, BLK, 2 * BLK, BLK, DILATED_PAIRS[1][1], DILATED_PAIRS[1][0] // DILATED_PAIRS[1][1]),
            _tile_bias(slopes_b, seq // 16, seq // 16, 0, DILATED_PAIRS[2][1],
                       DILATED_PAIRS[2][0] // DILATED_PAIRS[2][1]))
    biases = _sample_bias(slopes_a, nq, la, WIN_A - 1, False) + _sample_bias(slopes_b, nq, lb, W_MAX, True)

    cak, cav, cbk, cbv = (_seq_minor(c) for c in (cache_a_k, cache_a_v, cache_b_k, cache_b_v))

    w_in_b = w_in.astype(BF16)
    kv0, kv1 = QA_W, QA_W + KVA_W
    kb0 = kv1 + HB_W
    w_in_t = jnp.transpose(jnp.concatenate([w_in_b[:, :, kv0:kv1], w_in_b[:, :, kb0:kb0 + 2 * HB_W]], axis=2),
                           (0, 2, 1))
    wa_b, wb_b, wo_b = w_branch_a.astype(BF16), w_branch_b.astype(BF16), w_out.astype(BF16)
    fg_b, fu_b, fd_b = ffn_w_gate.astype(BF16), ffn_w_up.astype(BF16), ffn_w_down.astype(BF16)
    mg_b, mu_b, md_b = moe_w_gate.astype(BF16), moe_w_up.astype(BF16), moe_w_down.astype(BF16)
    router_pad = jnp.pad(moe_router.astype(F32), ((0, 0), (0, 0), (0, LANES - N_EXPERTS)))
    router_hi = router_pad.astype(BF16)
    router_lo = (router_pad - router_hi.astype(F32)).astype(BF16)

    x = jnp.concatenate([x_prompt.reshape(mp, D_MODEL), x_sample.reshape(ms, D_MODEL)], axis=0)
    xb = x.astype(BF16)

    pak, pav, sak, sav = [], [], [], []
    chain = prompt_chain = None
    for l in range(depth):
        qa, kva, qb, kb, vb, g, kat, vat, kbt, vbt = _inproj(xb, w_in_b[l], w_in_t[l], bp, seq, l, depth,
                                                             prompt_chain)
        prompt_chain = (kbt, vbt)
        oa_p, ob_p = _attn_prompt(attn_sink[l], qa, qb, kva, kb, vb, tabs, bp, seq)
        oa_s, ob_s, nak, nav, nbk, nbv = _attn_sample(l, depth, attn_sink[l], qa, qb, kva, kb, vb,
                                                      cak, cav, cbk, cbv, biases, mp, nb, nq, chain)
        chain = (nbk, nbv)
        pak.append(kat[:, :, seq - WIN_A:])
        pav.append(vat[:, :, seq - WIN_A:])
        sak.append(nak)
        sav.append(nav)

        ln1 = (ln1_g[l].reshape(1, D_MODEL), ln1_b[l].reshape(1, D_MODEL))
        ln2 = (ln2_g[l].reshape(1, D_MODEL), ln2_b[l].reshape(1, D_MODEL))
        i = l // 2
        if l % 2 == 0:
            x1, x1b = _post(x, oa_p, ob_p, oa_s, ob_s, g, wa_b[l], wb_b[l], wo_b[l], *ln1, alpha)
            x, xb = _ffn(x1, x1b, fg_b[i], fu_b[i], fd_b[i], *ln2, alpha)
        else:
            x1, x1b, comb = _post(x, oa_p, ob_p, oa_s, ob_s, g, wa_b[l], wb_b[l], wo_b[l], *ln1, alpha,
                                  (router_hi[i], router_lo[i]))
            x, xb = _moe(i, x1, x1b, comb, mg_b, mu_b, md_b, *ln2, alpha)

    return (x[:mp].reshape(bp, seq, D_MODEL), x[mp:].reshape(nb, nq, D_MODEL),
            _seq_major(jnp.stack(pak), KV_A), _seq_major(jnp.stack(pav), KV_A),
            _seq_major(prompt_chain[0], H_B), _seq_major(prompt_chain[1], H_B),
            _seq_major(jnp.stack(sak), KV_A), _seq_major(jnp.stack(sav), KV_A),
            _seq_major(chain[0], H_B), _seq_major(chain[1], H_B))
```

```python
import functools
import math

import jax
import jax.numpy as jnp
from jax import lax
from jax.experimental import pallas as pl
from jax.experimental.pallas import tpu as pltpu

F32 = jnp.float32
BF16 = jnp.bfloat16

D_MODEL = 1024
HEAD_DIM = 64
H_A = 8
KV_A = 2
WIN_A = 128
H_B = 8
DILATED_PAIRS = ((128, 1), (512, 4), (2048, 16))
W_MAX = 2048
N_EXPERTS = 8
LN_EPS = 1e-5
NEG_INF = -1e30
QA_W = H_A * HEAD_DIM
KVA_W = 2 * KV_A * HEAD_DIM
HB_W = H_B * HEAD_DIM
IN_COLS = QA_W + KVA_W + 3 * HB_W + 2 * D_MODEL
LANES = 128
VMEM_LIMIT_BYTES = 56 * 1024 * 1024


def _params(*sem):
    return pltpu.CompilerParams(dimension_semantics=sem, vmem_limit_bytes=VMEM_LIMIT_BYTES)


def _row_tile(m, cap):
    t = cap
    while m % t:
        t //= 2
    assert t >= 8, (m, cap)
    return t


def _sigmoid(x):
    return 1.0 / (1.0 + jnp.exp(-x))


def _layer_norm(z, g, b):
    mu = jnp.mean(z, axis=-1, keepdims=True)
    zc = z - mu
    var = jnp.mean(zc * zc, axis=-1, keepdims=True)
    return zc * lax.rsqrt(var + LN_EPS) * g + b


def _dot(a, b):
    return jnp.dot(a, b, preferred_element_type=F32)


def _dot_nt(a, b):
    return lax.dot_general(a, b, (((1,), (1,)), ((), ())), preferred_element_type=F32)


def _inproj_kernel(*refs, prompt_tiles, chained):
    x_ref, w_ref = refs[:2]
    (qa_ref, kva_ref, qb_ref, kb_ref, vb_ref, g_ref,
     kat_ref, vat_ref, kbt_ref, vbt_ref) = refs[2 + (2 if chained else 0):]
    x = x_ref[...]
    scale = HEAD_DIM ** -0.5

    def mm(c0, c1):
        return _dot(x, w_ref[:, c0:c1])

    c = 0
    qa_ref[...] = mm(c, c + QA_W) * scale
    c += QA_W
    kva = mm(c, c + KVA_W)
    kva_ref[...] = kva
    c += KVA_W
    qb_ref[...] = mm(c, c + HB_W) * scale
    c += HB_W
    kb = mm(c, c + HB_W)
    kb_ref[...] = kb
    c += HB_W
    vb = mm(c, c + HB_W)
    vb_ref[...] = vb
    c += HB_W
    g_ref[...] = mm(c, c + 2 * D_MODEL)

    @pl.when(pl.program_id(0) < prompt_tiles)
    def _seq_minor_outputs():
        ga = KV_A * HEAD_DIM
        kat_ref[...] = kva[:, 0:ga].T
        vat_ref[...] = kva[:, ga:2 * ga].T
        kbt_ref[...] = kb.T
        vbt_ref[...] = vb.T


def _inproj(xb, w, bp, seq, layer, depth, chain):
    m = xb.shape[0]
    tm = _row_tile(math.gcd(m, seq), 512)
    tiles_per_seq = seq // tm
    prompt_tiles = bp * tiles_per_seq
    widths = (QA_W, KVA_W, HB_W, HB_W, HB_W, 2 * D_MODEL)
    ga = KV_A * HEAD_DIM

    def seq_tile(i):
        j = jnp.minimum(i, prompt_tiles - 1)
        return j // tiles_per_seq, j % tiles_per_seq

    def per_layer(rows):
        def index(i):
            b, j = seq_tile(i)
            return (b, 0, j)
        return pl.BlockSpec((None, rows, tm), index)

    def stacked(rows):
        def index(i):
            b, j = seq_tile(i)
            return (layer, b, 0, j)
        return pl.BlockSpec((None, None, rows, tm), index)

    in_specs = [pl.BlockSpec((tm, D_MODEL), lambda i: (i, 0)),
                pl.BlockSpec((D_MODEL, IN_COLS), lambda i: (0, 0))]
    args = [xb, w]
    aliases = {}
    if chain is not None:
        in_specs += [pl.BlockSpec(memory_space=pl.ANY)] * 2
        aliases = {2: 8, 3: 9}
        args += list(chain)
    return pl.pallas_call(
        functools.partial(_inproj_kernel, prompt_tiles=prompt_tiles, chained=chain is not None),
        grid=(m // tm,),
        in_specs=in_specs,
        out_specs=[pl.BlockSpec((tm, wd), lambda i: (i, 0)) for wd in widths]
        + [per_layer(ga), per_layer(ga), stacked(HB_W), stacked(HB_W)],
        out_shape=[jax.ShapeDtypeStruct((m, wd), F32) for wd in widths]
        + [jax.ShapeDtypeStruct((bp, ga, seq), F32)] * 2 + [jax.ShapeDtypeStruct((depth, bp, HB_W, seq), F32)] * 2,
        input_output_aliases=aliases,
        compiler_params=_params("arbitrary"),
        name="inproj",
    )(*args)


BLK = 128
HEADS_PER_STEP = 2


def _tile_bias(slopes, n_q, n_k, shift, dist_scale, max_sub):
    r = jnp.arange(n_q, dtype=jnp.int32)[:, None]
    c = jnp.arange(n_k, dtype=jnp.int32)[None, :]
    dsub = shift + r - c
    ok = (dsub >= 0) & (dsub <= max_sub)
    b = -slopes[:, None, None] * (dist_scale * dsub).astype(F32)[None]
    full = jnp.where(ok[None], b, NEG_INF)
    first = jnp.where((ok & (c >= shift))[None], b, NEG_INF)
    return jnp.stack([full, first], axis=1).astype(F32)


def _pair_tile(q_s, rows, k, v, bias_ref, first, sinks=None):
    low = lax.broadcasted_iota(jnp.int32, (1, HEADS_PER_STEP * HEAD_DIM), 1) < HEAD_DIM
    pv, inv, lse = [], [], []
    for hh in range(HEADS_PER_STEP):
        s = _dot_nt(q_s[hh, rows, :], k) + bias_ref[hh, first]
        m = jnp.max(s, axis=-1, keepdims=True)
        if sinks is not None:
            m = jnp.maximum(m, sinks[hh])
        p = jnp.exp(s - m)
        l = jnp.sum(p, axis=-1, keepdims=True)
        if sinks is not None:
            l = l + jnp.exp(sinks[hh] - m)
        pv.append(_dot(p.astype(BF16), v))
        inv.append(1.0 / l)
        lse.append(m + jnp.log(l))
    return jnp.where(low, pv[0], pv[1]) * jnp.where(low, inv[0], inv[1]), jnp.where(low, lse[0], lse[1])


def _attn_prompt_kernel(sink_ref, qa_ref, qb_ref, kva_ref, kb_ref, vb_ref, ta_ref, t1_ref, t2_ref, t3_ref,
                        oa_ref, ob_ref,
                        qa_s, ka_s, va_s, q1_s, k1_s, v1_s, q4_s, k4_s, v4_s, q16_s, k16_s, v16_s, ro_s, rl_s,
                        *, seq):
    hp = pl.program_id(1)
    nblk = seq // BLK
    n4, n16 = seq // 4, seq // 16
    sub4 = n4 + BLK
    hd = HEAD_DIM
    low = lax.broadcasted_iota(jnp.int32, (1, HEADS_PER_STEP * hd), 1) < hd

    @pl.when((pl.program_id(0) == 0) & (hp == 0))
    def _zero_key_pads():
        for ref in (ka_s, va_s, k1_s, v1_s, k4_s, v4_s):
            ref[...] = jnp.zeros(ref.shape, BF16)

    def put_queries(q_s, dst, q):
        q_s[0, dst, :] = jnp.where(low, q, 0.0).astype(BF16)
        q_s[1, dst, :] = jnp.where(low, 0.0, q).astype(BF16)

    group0 = (hp // (H_A // KV_A // HEADS_PER_STEP)) == 0

    def both_halves(x):
        return jnp.where(low == group0, x, pltpu.roll(x, hd, axis=1)).astype(BF16)

    def stage_natural(c, carry):
        r0 = pl.multiple_of(c * BLK, BLK)
        rows, prows = pl.ds(r0, BLK), pl.ds(r0 + BLK, BLK)
        kva = kva_ref[rows, :]
        ka_s[prows, :] = both_halves(kva[:, 0:2 * hd])
        va_s[prows, :] = both_halves(kva[:, 2 * hd:4 * hd])
        put_queries(qa_s, rows, qa_ref[rows, :])
        put_queries(q1_s, rows, qb_ref[rows, :])
        k1_s[prows, :] = kb_ref[rows, :].astype(BF16)
        v1_s[prows, :] = vb_ref[rows, :].astype(BF16)
        return carry
    lax.fori_loop(0, nblk, stage_natural, 0, unroll=4)

    for r4 in range(4):
        def stage_mod4(c, carry, r4=r4):
            j0 = pl.multiple_of(c * BLK, BLK)
            src = pl.ds(r4 + 4 * j0, BLK, stride=4)
            put_queries(q4_s, pl.ds(r4 * n4 + j0, BLK), qb_ref[src, :])
            k4_s[pl.ds(r4 * sub4 + BLK + j0, BLK), :] = kb_ref[src, :].astype(BF16)
            v4_s[pl.ds(r4 * sub4 + BLK + j0, BLK), :] = vb_ref[src, :].astype(BF16)
            return carry
        lax.fori_loop(0, n4 // BLK, stage_mod4, 0, unroll=True)

    def stage_mod16(r, carry):
        src = pl.ds(r, n16, stride=16)
        dst = pl.ds(pl.multiple_of(r * n16, n16), n16)
        put_queries(q16_s, dst, qb_ref[src, :])
        k16_s[dst, :] = kb_ref[src, :].astype(BF16)
        v16_s[dst, :] = vb_ref[src, :].astype(BF16)
        return carry
    lax.fori_loop(0, 16, stage_mod16, 0, unroll=4)

    sinks = [sink_ref[HEADS_PER_STEP * hp + hh] for hh in range(HEADS_PER_STEP)]

    def mixer_a(i, carry):
        r0 = pl.multiple_of(i * BLK, BLK)
        o, _ = _pair_tile(qa_s, pl.ds(r0, BLK), ka_s[pl.ds(r0, 2 * BLK), :], va_s[pl.ds(r0, 2 * BLK), :],
                          ta_ref, jnp.where(i == 0, 1, 0), sinks)
        oa_ref[pl.ds(r0, BLK), :] = o.astype(BF16)
        return carry
    lax.fori_loop(0, nblk, mixer_a, 0, unroll=True)

    def keep(pair, dst, o_lse):
        ro_s[pair, dst, :] = o_lse[0]
        rl_s[pair, dst, :] = jnp.broadcast_to(o_lse[1], o_lse[0].shape)

    def pair_w128(i, carry):
        r0 = pl.multiple_of(i * BLK, BLK)
        keep(0, pl.ds(r0, BLK), _pair_tile(q1_s, pl.ds(r0, BLK), k1_s[pl.ds(r0, 2 * BLK), :],
                                          v1_s[pl.ds(r0, 2 * BLK), :], t1_ref, jnp.where(i == 0, 1, 0)))
        return carry
    lax.fori_loop(0, nblk, pair_w128, 0, unroll=True)

    for r4 in range(4):
        def pair_w512(i, carry, r4=r4):
            j0 = pl.multiple_of(i * BLK, BLK)
            keep(1, pl.ds(r4 + 4 * j0, BLK, stride=4),
                 _pair_tile(q4_s, pl.ds(r4 * n4 + j0, BLK), k4_s[pl.ds(r4 * sub4 + j0, 2 * BLK), :],
                            v4_s[pl.ds(r4 * sub4 + j0, 2 * BLK), :], t2_ref, jnp.where(i == 0, 1, 0)))
            return carry
        lax.fori_loop(0, n4 // BLK, pair_w512, 0, unroll=True)

    def pair_w2048(r, carry):
        src = pl.ds(pl.multiple_of(r * n16, n16), n16)
        keep(2, pl.ds(r, n16, stride=16), _pair_tile(q16_s, src, k16_s[src, :], v16_s[src, :], t3_ref, 0))
        return carry
    lax.fori_loop(0, 16, pair_w2048, 0, unroll=True)

    def merge(i, carry):
        rows = pl.ds(pl.multiple_of(i * BLK, BLK), BLK)
        l0, l1, l2 = rl_s[0, rows, :], rl_s[1, rows, :], rl_s[2, rows, :]
        mx = jnp.maximum(l0, jnp.maximum(l1, l2))
        w0, w1, w2 = jnp.exp(l0 - mx), jnp.exp(l1 - mx), jnp.exp(l2 - mx)
        o = (w0 * ro_s[0, rows, :] + w1 * ro_s[1, rows, :] + w2 * ro_s[2, rows, :]) / (w0 + w1 + w2)
        ob_ref[rows, :] = o.astype(BF16)
        return carry
    lax.fori_loop(0, nblk, merge, 0, unroll=4)


def _attn_prompt(sink, qa, qb, kva, kb, vb, tabs, batch, seq):
    mp = batch * seq
    n4 = seq // 4
    lanes = HEADS_PER_STEP * HEAD_DIM
    smem = pl.BlockSpec(memory_space=pltpu.SMEM)
    cols = pl.BlockSpec((seq, lanes), lambda b, hp: (b, hp))

    def tab(t):
        return pl.BlockSpec((HEADS_PER_STEP,) + t.shape[1:], lambda b, hp: (hp, 0, 0, 0))

    def masked_q():
        return pltpu.VMEM((HEADS_PER_STEP, seq, lanes), BF16)

    def packed(rows):
        return pltpu.VMEM((rows, lanes), BF16)

    return pl.pallas_call(
        functools.partial(_attn_prompt_kernel, seq=seq),
        grid=(batch, H_B // HEADS_PER_STEP),
        in_specs=[smem, cols, cols, pl.BlockSpec((seq, KVA_W), lambda b, hp: (b, 0)), cols, cols]
        + [tab(t) for t in tabs],
        out_specs=[cols, cols],
        out_shape=[jax.ShapeDtypeStruct((mp, QA_W), BF16), jax.ShapeDtypeStruct((mp, HB_W), BF16)],
        scratch_shapes=[masked_q(), packed(seq + BLK), packed(seq + BLK),
                        masked_q(), packed(seq + BLK), packed(seq + BLK),
                        masked_q(), packed(4 * (n4 + BLK)), packed(4 * (n4 + BLK)),
                        masked_q(), packed(seq), packed(seq),
                        pltpu.VMEM((3, seq, lanes), F32), pltpu.VMEM((3, seq, lanes), F32)],
        compiler_params=_params("arbitrary", "arbitrary"),
        name="attn_prompt",
    )(sink, qa, qb, kva, kb, vb, *tabs)


def _sample_bias(slopes, nq, cache_len, max_dist, dilated):
    i = jnp.arange(nq, dtype=jnp.int32)[:, None]

    def make(dist, real):
        if dilated:
            cnt = jnp.zeros(dist.shape, jnp.int32)
            for w, d in DILATED_PAIRS:
                cnt = cnt + jnp.where(((dist & (d - 1)) == 0) & (dist <= w), 1, 0)
            ok = real & (dist >= 0) & (cnt > 0)
            extra = jnp.where(cnt == 3, math.log(3.0), jnp.where(cnt == 2, math.log(2.0), 0.0)).astype(F32)
        else:
            ok = real & (dist >= 0) & (dist <= max_dist)
            extra = jnp.zeros(dist.shape, F32)
        bias = -slopes[:, None, None] * dist.astype(F32)[None] + extra[None]
        return jnp.where(ok[None], bias, NEG_INF).reshape(slopes.shape[0] * nq, dist.shape[1]).astype(F32)

    key = jnp.arange(cache_len, dtype=jnp.int32)[None, :]
    new = jnp.arange(LANES, dtype=jnp.int32)[None, :] - (LANES - nq)
    return make(cache_len + i - key, key >= 0), make(i - new, new >= 0)


def _attn_sample_kernel(*refs, nq, la, lb, chained):
    (sink_ref, qa_ref, qb_ref, kva_ref, kb_ref, vb_ref, cak_ref, cav_ref, cbk_ref, cbv_ref,
     bac_ref, ban_ref, bbc_ref, bbn_ref) = refs[:14]
    refs = refs[14 + (2 if chained else 0):]
    oa_ref, ob_ref, nak_ref, nav_ref, nbk_ref, nbv_ref, s_s, p_s = refs
    rows = H_A * nq
    hd = HEAD_DIM
    ga = KV_A * hd
    rep = H_A // KV_A

    def new_rows(x):
        return jnp.concatenate([jnp.zeros((LANES - nq, x.shape[1]), F32), x], axis=0)

    lane = lax.broadcasted_iota(jnp.int32, (1, LANES), 1)

    def shifted(cache_ref, out_ref, new_t, length, row_chunk):
        for r0 in range(0, cache_ref.shape[0], row_chunk):
            rs = slice(r0, r0 + row_chunk)
            rolled = pltpu.roll(cache_ref[rs, :], length - nq, axis=1)
            if length > LANES:
                out_ref[rs, 0:length - LANES] = rolled[:, 0:length - LANES]
            out_ref[rs, length - LANES:length] = jnp.where(lane < LANES - nq, rolled[:, length - LANES:length],
                                                           new_t[rs, :])

    qa = qa_ref[...]
    kva = kva_ref[...]
    zeros = jnp.zeros((nq, hd), F32)
    qbd_a = jnp.concatenate(
        [jnp.concatenate([qa[:, h * hd:(h + 1) * hd] if g == h // rep else zeros for g in range(KV_A)], axis=1)
         for h in range(H_A)], axis=0).astype(BF16)
    kn_a = new_rows(kva[:, 0:ga])
    vn_a = new_rows(kva[:, ga:2 * ga])
    hrow = lax.broadcasted_iota(jnp.int32, (rows, 1), 0) // nq
    sink = jnp.zeros((rows, 1), F32)
    for h in range(H_A):
        sink = jnp.where(hrow == h, sink_ref[h], sink)
    s_c = _dot(qbd_a, cak_ref[...].astype(BF16)) + bac_ref[...]
    s_n = _dot_nt(qbd_a, kn_a.astype(BF16)) + ban_ref[...]
    m = jnp.maximum(jnp.maximum(jnp.max(s_c, axis=-1, keepdims=True), jnp.max(s_n, axis=-1, keepdims=True)), sink)
    p_c, p_n = jnp.exp(s_c - m), jnp.exp(s_n - m)
    l = jnp.sum(p_c, axis=-1, keepdims=True) + jnp.sum(p_n, axis=-1, keepdims=True) + jnp.exp(sink - m)
    o = (_dot_nt(p_c.astype(BF16), cav_ref[...].astype(BF16)) + _dot(p_n.astype(BF16), vn_a.astype(BF16))) / l
    oa_ref[...] = jnp.concatenate(
        [o[h * nq:(h + 1) * nq, (h // rep) * hd:(h // rep + 1) * hd] for h in range(H_A)], axis=1)
    shifted(cak_ref, nak_ref, kn_a.T, la, ga)
    shifted(cav_ref, nav_ref, vn_a.T, la, ga)

    r = lax.broadcasted_iota(jnp.int32, (rows, HB_W), 0)
    c = lax.broadcasted_iota(jnp.int32, (rows, HB_W), 1)
    own = (r // nq) == (c // hd)
    qbd = jnp.where(own, jnp.concatenate([qb_ref[...]] * H_B, axis=0), 0.0).astype(BF16)
    kn_b = new_rows(kb_ref[...])
    vn_b = new_rows(vb_ref[...])
    chunk = 512
    for c0 in range(0, lb, chunk):
        cs = slice(c0, c0 + chunk)
        s_s[:, cs] = _dot(qbd, cbk_ref[:, cs].astype(BF16)) + bbc_ref[:, cs]
    s_n = _dot_nt(qbd, kn_b.astype(BF16)) + bbn_ref[...]
    s_c = s_s[...]
    m = jnp.maximum(jnp.max(s_c, axis=-1, keepdims=True), jnp.max(s_n, axis=-1, keepdims=True))
    p_c, p_n = jnp.exp(s_c - m), jnp.exp(s_n - m)
    l = jnp.sum(p_c, axis=-1, keepdims=True) + jnp.sum(p_n, axis=-1, keepdims=True)
    p_s[...] = p_c.astype(BF16)
    o = _dot(p_n.astype(BF16), vn_b.astype(BF16))
    for c0 in range(0, lb, chunk):
        cs = slice(c0, c0 + chunk)
        o = o + _dot_nt(p_s[:, cs], cbv_ref[:, cs].astype(BF16))
    o = jnp.where(own, o, 0.0) / l
    out = o[0:nq]
    for h in range(1, H_B):
        out = out + o[h * nq:(h + 1) * nq]
    ob_ref[...] = out
    shifted(cbk_ref, nbk_ref, kn_b.T, lb, 64)
    shifted(cbv_ref, nbv_ref, vn_b.T, lb, 64)


def _attn_sample(layer, depth, sink, qa, qb, kva, kb, vb, cak, cav, cbk, cbv, biases, row0, nb, nq, chain):
    la, lb = cak.shape[3], cbk.shape[3]
    ga = KV_A * HEAD_DIM
    rows = H_A * nq
    blk0 = row0 // nq
    smem = pl.BlockSpec(memory_space=pltpu.SMEM)

    def tok(wd):
        return pl.BlockSpec((nq, wd), lambda b: (blk0 + b, 0))

    def cache(r, length):
        return pl.BlockSpec((None, None, r, length), lambda b: (layer, b, 0, 0))

    def const(a):
        return pl.BlockSpec(a.shape, lambda b: (0, 0))

    in_specs = [smem, tok(QA_W), tok(HB_W), tok(KVA_W), tok(HB_W), tok(HB_W),
                cache(ga, la), cache(ga, la), cache(HB_W, lb), cache(HB_W, lb)] + [const(a) for a in biases]
    args = [sink, qa, qb, kva, kb, vb, cak, cav, cbk, cbv, *biases]
    aliases = {}
    if chain is not None:
        in_specs += [pl.BlockSpec(memory_space=pl.ANY)] * 2
        aliases = {len(args): 4, len(args) + 1: 5}
        args += list(chain)
    out_tok = pl.BlockSpec((nq, QA_W), lambda b: (b, 0))
    return pl.pallas_call(
        functools.partial(_attn_sample_kernel, nq=nq, la=la, lb=lb, chained=chain is not None),
        grid=(nb,),
        in_specs=in_specs,
        out_specs=[out_tok, out_tok,
                   pl.BlockSpec((None, ga, la), lambda b: (b, 0, 0)), pl.BlockSpec((None, ga, la), lambda b: (b, 0, 0)),
                   cache(HB_W, lb), cache(HB_W, lb)],
        out_shape=[jax.ShapeDtypeStruct((nb * nq, QA_W), F32), jax.ShapeDtypeStruct((nb * nq, HB_W), F32),
                   jax.ShapeDtypeStruct((nb, ga, la), F32), jax.ShapeDtypeStruct((nb, ga, la), F32),
                   jax.ShapeDtypeStruct((depth, nb, HB_W, lb), F32), jax.ShapeDtypeStruct((depth, nb, HB_W, lb), F32)],
        scratch_shapes=[pltpu.VMEM((rows, lb), F32), pltpu.VMEM((rows, lb), BF16)],
        input_output_aliases=aliases,
        compiler_params=_params("arbitrary"),
        name="attn_sample",
    )(*args)


def _post_kernel(*refs, alpha, with_router, prompt_tiles):
    if with_router:
        (x_ref, oap_ref, obp_ref, oas_ref, obs_ref, g_ref, wa_ref, wb_ref, wo_ref, lg_ref, lb_ref, wrh_ref, wrl_ref,
         x1_ref, x1b_ref, comb_ref) = refs
    else:
        (x_ref, oap_ref, obp_ref, oas_ref, obs_ref, g_ref, wa_ref, wb_ref, wo_ref, lg_ref, lb_ref,
         x1_ref, x1b_ref) = refs
    prompt = pl.program_id(0) < prompt_tiles
    oa = jnp.where(prompt, oap_ref[...], oas_ref[...].astype(BF16))
    ob = jnp.where(prompt, obp_ref[...], obs_ref[...].astype(BF16))
    a = _dot(oa, wa_ref[...])
    b = _dot(ob, wb_ref[...])
    g = g_ref[...]
    merged = _sigmoid(g[:, :D_MODEL]) * a + _sigmoid(g[:, D_MODEL:]) * b
    y = _dot(merged.astype(BF16), wo_ref[...])
    x1 = _layer_norm(alpha * x_ref[...] + y, lg_ref[...], lb_ref[...])
    x1b = x1.astype(BF16)
    x1_ref[...] = x1
    x1b_ref[...] = x1b
    if with_router:
        x1l = (x1 - x1b.astype(F32)).astype(BF16)
        logits = _dot(x1b, wrh_ref[...]) + (_dot(x1l, wrh_ref[...]) + _dot(x1b, wrl_ref[...]))
        lane = lax.broadcasted_iota(jnp.int32, logits.shape, 1)
        logits = jnp.where(lane < N_EXPERTS, logits, -jnp.inf)
        t1 = jnp.max(logits, axis=-1, keepdims=True)
        lane = lane.astype(F32)
        i1 = jnp.min(jnp.where(logits == t1, lane, float(LANES)), axis=-1, keepdims=True)
        rest = jnp.where(lane == i1, -jnp.inf, logits)
        t2 = jnp.max(rest, axis=-1, keepdims=True)
        i2 = jnp.min(jnp.where(rest == t2, lane, float(LANES)), axis=-1, keepdims=True)
        e2 = jnp.exp(t2 - t1)
        den = 1.0 + e2
        comb_ref[...] = jnp.where(lane == i1, 1.0 / den, jnp.where(lane == i2, e2 / den, 0.0))


def _post(x, oa_p, ob_p, oa_s, ob_s, g, wa, wb, wo, ln_g, ln_b, alpha, w_router=None):
    m = x.shape[0]
    tm = _row_tile(math.gcd(oa_p.shape[0], oa_s.shape[0]), 512)
    prompt_tiles = oa_p.shape[0] // tm
    with_router = w_router is not None

    def rows(wd):
        return pl.BlockSpec((tm, wd), lambda i: (i, 0))

    def prompt_rows(wd):
        return pl.BlockSpec((tm, wd), lambda i: (jnp.minimum(i, prompt_tiles - 1), 0))

    def sample_rows(wd):
        return pl.BlockSpec((tm, wd), lambda i: (jnp.maximum(i - prompt_tiles, 0), 0))

    def const(shape):
        return pl.BlockSpec(shape, lambda i: (0, 0))

    in_specs = [rows(D_MODEL), prompt_rows(QA_W), prompt_rows(HB_W), sample_rows(QA_W), sample_rows(HB_W),
                rows(2 * D_MODEL), const(wa.shape), const(wb.shape), const(wo.shape),
                const((1, D_MODEL)), const((1, D_MODEL))]
    args = [x, oa_p, ob_p, oa_s, ob_s, g, wa, wb, wo, ln_g, ln_b]
    out_specs = [rows(D_MODEL), rows(D_MODEL)]
    out_shape = [jax.ShapeDtypeStruct((m, D_MODEL), F32), jax.ShapeDtypeStruct((m, D_MODEL), BF16)]
    if with_router:
        in_specs += [const(w_router[0].shape), const(w_router[1].shape)]
        args += list(w_router)
        out_specs.append(rows(LANES))
        out_shape.append(jax.ShapeDtypeStruct((m, LANES), F32))
    return pl.pallas_call(
        functools.partial(_post_kernel, alpha=alpha, with_router=with_router, prompt_tiles=prompt_tiles),
        grid=(m // tm,),
        in_specs=in_specs, out_specs=out_specs, out_shape=out_shape,
        compiler_params=_params("arbitrary"),
        name="post_router" if with_router else "post",
    )(*args)


def _swiglu_chunk(xb, wg, wu, wd):
    hg = _dot(xb, wg)
    hu = _dot(xb, wu)
    return _dot((hg * _sigmoid(hg) * hu).astype(BF16), wd)


def _ffn_kernel(x_ref, xb_ref, wg_ref, wu_ref, wd_ref, lg_ref, lb_ref, x2_ref, x2b_ref, acc_ref,
                *, alpha, nf):
    f = pl.program_id(1)

    @pl.when(f == 0)
    def _():
        acc_ref[...] = jnp.zeros(acc_ref.shape, F32)

    acc_ref[...] += _swiglu_chunk(xb_ref[...], wg_ref[...], wu_ref[...], wd_ref[...])

    @pl.when(f == nf - 1)
    def _():
        x2 = _layer_norm(alpha * x_ref[...] + acc_ref[...], lg_ref[...], lb_ref[...])
        x2_ref[...] = x2
        x2b_ref[...] = x2.astype(BF16)


def _ffn(x, xb, wg, wu, wd, ln_g, ln_b, alpha):
    m = x.shape[0]
    tm = _row_tile(m, 512)
    d_ff = wg.shape[1]
    nf = 2
    tf = d_ff // nf
    rows = pl.BlockSpec((tm, D_MODEL), lambda i, f: (i, 0))
    vec = pl.BlockSpec((1, D_MODEL), lambda i, f: (0, 0))
    return pl.pallas_call(
        functools.partial(_ffn_kernel, alpha=alpha, nf=nf),
        grid=(m // tm, nf),
        in_specs=[rows, rows,
                  pl.BlockSpec((D_MODEL, tf), lambda i, f: (0, f)),
                  pl.BlockSpec((D_MODEL, tf), lambda i, f: (0, f)),
                  pl.BlockSpec((tf, D_MODEL), lambda i, f: (f, 0)),
                  vec, vec],
        out_specs=[rows, rows],
        out_shape=[jax.ShapeDtypeStruct((m, D_MODEL), F32), jax.ShapeDtypeStruct((m, D_MODEL), BF16)],
        scratch_shapes=[pltpu.VMEM((tm, D_MODEL), F32)],
        compiler_params=_params("arbitrary", "arbitrary"),
        name="ffn",
    )(x, xb, wg, wu, wd, ln_g, ln_b)


MOE_TILE = 512
MOE_SUB = 256
FLAG_FIRST, FLAG_LAST, FLAG_ACTIVE = 1, 2, 4


def _steps(n, lo, total_steps):
    end = jnp.cumsum(n)
    start = end - n
    s = jnp.arange(total_steps, dtype=jnp.int32)
    sc = jnp.minimum(s, end[-1] - 1)
    g = jnp.searchsorted(end, sc, side="right", method="compare_all").astype(jnp.int32)
    return g, (lo[g] + sc - start[g]).astype(jnp.int32), sc, s < end[-1], start, end


def _step_flags(active, first, last):
    return (jnp.where(active & first, FLAG_FIRST, 0) + jnp.where(active & last, FLAG_LAST, 0)
            + jnp.where(active, FLAG_ACTIVE, 0)).astype(jnp.int32)


def _route_plan(comb, tm):
    m = comb.shape[0]
    onehot = comb[:, :N_EXPERTS] > 0.0
    oh = onehot.astype(jnp.int32)
    csum = jnp.cumsum(oh, axis=0)
    cnt = csum[-1]
    tiles_e = (cnt + tm - 1) // tm
    tile_end = jnp.cumsum(tiles_e)
    tile_off = tile_end - tiles_e
    pos = jnp.where(onehot, (tile_off * tm)[None, :] + csum - oh, -1)
    nt = 2 * m // tm + N_EXPERTS
    n_active = tile_end[-1].astype(jnp.int32)
    tiles = jnp.arange(nt, dtype=jnp.int32)
    tile_expert = jnp.minimum(jnp.searchsorted(tile_end, tiles, side="right", method="compare_all"),
                              N_EXPERTS - 1).astype(jnp.int32)

    ts = MOE_SUB
    per = tm // ts
    subs = jnp.arange(nt * per, dtype=jnp.int32)
    sub_expert = tile_expert[subs // per]
    k0 = (subs - tile_off[sub_expert] * per) * ts
    k1 = jnp.minimum(k0 + ts, cnt[sub_expert]) - 1
    find = jax.vmap(lambda col, q: jnp.searchsorted(col, q, side="left", method="compare_all"))

    def token_of_rank(q):
        every = find(csum.T, jnp.broadcast_to(q[None, :], (N_EXPERTS, nt * per)))
        return jnp.take_along_axis(every, sub_expert[None, :], axis=0)[0]

    t_lo, t_hi = token_of_rank(k0 + 1), token_of_rank(k1 + 1)
    has_rows = k1 >= k0
    c_lo = jnp.where(has_rows, t_lo // tm, 0).astype(jnp.int32)
    n = jnp.where(subs < n_active * per, jnp.where(has_rows, t_hi // tm - t_lo // tm + 1, 1), 0).astype(jnp.int32)
    g, chunk, sc, active, start, end = _steps(n, c_lo, nt * per + N_EXPERTS * (m // tm))
    dispatch = (g, sub_expert[g], chunk, _step_flags(active, sc == start[g], sc == end[g] - 1))

    nti = m // tm
    posr = pos.reshape(nti, tm, N_EXPERTS)
    p_hi = jnp.max(posr, axis=1)
    p_lo = jnp.min(jnp.where(posr >= 0, posr, nt * tm), axis=1)
    c_lo = (p_lo // ts).reshape(-1).astype(jnp.int32)
    n = jnp.where(p_hi >= 0, p_hi // ts - p_lo // ts + 1, 0).reshape(-1).astype(jnp.int32)
    g, chunk, sc, active, start, end = _steps(n, c_lo, nti * N_EXPERTS + nt * per)
    ti, ex = g // N_EXPERTS, g % N_EXPERTS
    t_start = start.reshape(nti, N_EXPERTS)[:, 0]
    t_end = end.reshape(nti, N_EXPERTS)[:, -1]
    combine = (ti.astype(jnp.int32), ex.astype(jnp.int32), chunk,
               _step_flags(active, sc == t_start[ti], sc == t_end[ti] - 1))
    return pos.astype(F32), tile_expert, n_active.reshape(1), dispatch, combine


def _accumulate(flags, acc_ref, v):
    @pl.when((flags & FLAG_FIRST) != 0)
    def _():
        acc_ref[...] = v

    @pl.when((flags & FLAG_FIRST) == 0)
    def _():
        acc_ref[...] += v


def _dispatch_kernel(tile_ref, ex_ref, chunk_ref, flag_ref, pos_ref, x_ref, o_ref, acc_ref):
    s = pl.program_id(0)
    flags = flag_ref[s]

    @pl.when((flags & FLAG_ACTIVE) != 0)
    def _():
        ts, tc = acc_ref.shape[0], x_ref.shape[0]
        row = (lax.broadcasted_iota(jnp.int32, (ts, tc), 0) + tile_ref[s] * ts).astype(F32)
        pick = jnp.where(pos_ref[pl.ds(ex_ref[s], 1), :] == row, 1.0, 0.0).astype(BF16)
        _accumulate(flags, acc_ref, _dot(pick, x_ref[...]))

        @pl.when((flags & FLAG_LAST) != 0)
        def _():
            o_ref[...] = acc_ref[...].astype(BF16)


def _dispatch(xb, pos_t, plan, n_tiles, tm):
    tile, ex, chunk, flags = plan
    return pl.pallas_call(
        _dispatch_kernel,
        grid_spec=pltpu.PrefetchScalarGridSpec(
            num_scalar_prefetch=4,
            grid=(tile.shape[0],),
            in_specs=[pl.BlockSpec((N_EXPERTS, tm), lambda s, t, e, c, f: (0, c[s])),
                      pl.BlockSpec((tm, D_MODEL), lambda s, t, e, c, f: (c[s], 0))],
            out_specs=pl.BlockSpec((MOE_SUB, D_MODEL), lambda s, t, e, c, f: (t[s], 0)),
            scratch_shapes=[pltpu.VMEM((MOE_SUB, D_MODEL), F32)]),
        out_shape=jax.ShapeDtypeStruct((n_tiles * tm, D_MODEL), BF16),
        compiler_params=_params("arbitrary"),
        name="moe_dispatch",
    )(tile, ex, chunk, flags, pos_t, xb)


def _experts_kernel(texp_ref, nact_ref, xs_ref, wg_ref, wu_ref, wd_ref, ys_ref, acc_ref, *, nf):
    j = pl.program_id(0)
    f = pl.program_id(1)

    @pl.when(j < nact_ref[0])
    def _():
        @pl.when(f == 0)
        def _():
            acc_ref[...] = jnp.zeros(acc_ref.shape, F32)

        acc_ref[...] += _swiglu_chunk(xs_ref[...], wg_ref[...], wu_ref[...], wd_ref[...])

        @pl.when(f == nf - 1)
        def _():
            ys_ref[...] = acc_ref[...].astype(BF16)


def _experts(layer_moe, xs, tile_expert, n_active, wg, wu, wd, tm):
    nt = xs.shape[0] // tm
    nf = 2
    tf = wg.shape[3] // nf

    def row_map(j, f, te, na):
        return (jnp.minimum(j, na[0] - 1), 0)

    def col_f(j, f, na):
        return jnp.where(j < na[0], f, nf - 1)

    def expert(j, te, na):
        return te[jnp.minimum(j, na[0] - 1)]

    return pl.pallas_call(
        functools.partial(_experts_kernel, nf=nf),
        grid_spec=pltpu.PrefetchScalarGridSpec(
            num_scalar_prefetch=2,
            grid=(nt, nf),
            in_specs=[pl.BlockSpec((tm, D_MODEL), row_map),
                      pl.BlockSpec((None, None, D_MODEL, tf),
                                   lambda j, f, te, na: (layer_moe, expert(j, te, na), 0, col_f(j, f, na))),
                      pl.BlockSpec((None, None, D_MODEL, tf),
                                   lambda j, f, te, na: (layer_moe, expert(j, te, na), 0, col_f(j, f, na))),
                      pl.BlockSpec((None, None, tf, D_MODEL),
                                   lambda j, f, te, na: (layer_moe, expert(j, te, na), col_f(j, f, na), 0))],
            out_specs=pl.BlockSpec((tm, D_MODEL), row_map),
            scratch_shapes=[pltpu.VMEM((tm, D_MODEL), F32)]),
        out_shape=jax.ShapeDtypeStruct(xs.shape, BF16),
        compiler_params=_params("arbitrary", "arbitrary"),
        name="moe_experts",
    )(tile_expert, n_active, xs, wg, wu, wd)


def _combine_kernel(ti_ref, ex_ref, chunk_ref, flag_ref, x_ref, pos_ref, comb_ref, ys_ref, lg_ref, lb_ref,
                    x2_ref, x2b_ref, acc_ref, *, alpha):
    s = pl.program_id(0)
    flags = flag_ref[s]

    @pl.when((flags & FLAG_ACTIVE) != 0)
    def _():
        tt, tc = acc_ref.shape[0], ys_ref.shape[0]
        lane = lax.broadcasted_iota(jnp.int32, (tt, LANES), 1)
        mine = lane == ex_ref[s]
        row = jnp.max(jnp.where(mine, pos_ref[...], -1.0), axis=-1, keepdims=True)
        gate = jnp.sum(jnp.where(mine, comb_ref[...], 0.0), axis=-1, keepdims=True)
        col = (lax.broadcasted_iota(jnp.int32, (tt, tc), 1) + chunk_ref[s] * tc).astype(F32)
        pick = jnp.where(row == col, 1.0, 0.0).astype(BF16)
        _accumulate(flags, acc_ref, gate * _dot(pick, ys_ref[...]))

        @pl.when((flags & FLAG_LAST) != 0)
        def _():
            x2 = _layer_norm(alpha * x_ref[...] + acc_ref[...], lg_ref[...], lb_ref[...])
            x2_ref[...] = x2
            x2b_ref[...] = x2.astype(BF16)


def _combine(x, pos, comb, ys, plan, ln_g, ln_b, alpha, tm):
    ti, ex, chunk, flags = plan
    m = x.shape[0]
    pos_pad = jnp.pad(pos, ((0, 0), (0, LANES - N_EXPERTS)), constant_values=-1.0)

    def tok(wd):
        return pl.BlockSpec((tm, wd), lambda s, ti, ex, c, f: (ti[s], 0))

    vec = pl.BlockSpec((1, D_MODEL), lambda s, ti, ex, c, f: (0, 0))
    return pl.pallas_call(
        functools.partial(_combine_kernel, alpha=alpha),
        grid_spec=pltpu.PrefetchScalarGridSpec(
            num_scalar_prefetch=4,
            grid=(ti.shape[0],),
            in_specs=[tok(D_MODEL), tok(LANES), tok(LANES),
                      pl.BlockSpec((MOE_SUB, D_MODEL), lambda s, ti, ex, c, f: (c[s], 0)), vec, vec],
            out_specs=[tok(D_MODEL), tok(D_MODEL)],
            scratch_shapes=[pltpu.VMEM((tm, D_MODEL), F32)]),
        out_shape=[jax.ShapeDtypeStruct((m, D_MODEL), F32), jax.ShapeDtypeStruct((m, D_MODEL), BF16)],
        compiler_params=_params("arbitrary"),
        name="moe_combine",
    )(ti, ex, chunk, flags, x, pos_pad, comb, ys, ln_g, ln_b)


def _moe(layer_moe, x, xb, comb, wg, wu, wd, ln_g, ln_b, alpha):
    tm = MOE_TILE
    assert x.shape[0] % tm == 0
    pos, tile_expert, n_active, dispatch_plan, combine_plan = _route_plan(comb, tm)
    xs = _dispatch(xb, pos.T, dispatch_plan, tile_expert.shape[0], tm)
    ys = _experts(layer_moe, xs, tile_expert, n_active, wg, wu, wd, tm)
    return _combine(x, pos, comb, ys, combine_plan, ln_g, ln_b, alpha, tm)


def _seq_minor(cache):
    d, n, length, h, hd = cache.shape
    return jnp.transpose(cache, (0, 1, 3, 4, 2)).reshape(d, n, h * hd, length)


def _seq_major(x, heads):
    d, n, hw, length = x.shape
    return jnp.transpose(x.reshape(d, n, heads, hw // heads, length), (0, 1, 4, 2, 3))


def kernel(x_prompt, x_sample, cache_a_k, cache_a_v, cache_b_k, cache_b_v, w_in, attn_sink, w_branch_a,
           w_branch_b, w_out, ln1_g, ln1_b, ln2_g, ln2_b, ffn_w_gate, ffn_w_up, ffn_w_down, moe_router,
           moe_w_gate, moe_w_up, moe_w_down):
    depth = w_in.shape[0]
    bp, seq, _ = x_prompt.shape
    nb, nq, _ = x_sample.shape
    la, lb = cache_a_k.shape[2], cache_b_k.shape[2]
    assert seq % 512 == 0 and seq <= W_MAX and la == WIN_A and lb == W_MAX and nq == 8
    mp, ms = bp * seq, nb * nq
    alpha = (2.0 * depth) ** 0.25

    n_heads = H_A + H_B
    slopes = jnp.exp2(-8.0 * jnp.arange(1, n_heads + 1, dtype=F32) / n_heads)
    slopes_a, slopes_b = slopes[0::2], slopes[1::2]
    tabs = (_tile_bias(slopes_a, BLK, 2 * BLK, BLK, 1, WIN_A - 1),
            _tile_bias(slopes_b, BLK, 2 * BLK, BLK, DILATED_PAIRS[0][1], DILATED_PAIRS[0][0] // DILATED_PAIRS[0][1]),
            _tile_bias(slopes_b, BLK, 2 * BLK, BLK, DILATED_PAIRS[1][1], DILATED_PAIRS[1][0] // DILATED_PAIRS[1][1]),
            _tile_bias(slopes_b, seq // 16, seq // 16, 0, DILATED_PAIRS[2][1],
                       DILATED_PAIRS[2][0] // DILATED_PAIRS[2][1]))
    biases = _sample_bias(slopes_a, nq, la, WIN_A - 1, False) + _sample_bias(slopes_b, nq, lb, W_MAX, True)

    cak, cav, cbk, cbv = (_seq_minor(c) for c in (cache_a_k, cache_a_v, cache_b_k, cache_b_v))

    w_in_b = w_in.astype(BF16)
    wa_b, wb_b, wo_b = w_branch_a.astype(BF16), w_branch_b.astype(BF16), w_out.astype(BF16)
    fg_b, fu_b, fd_b = ffn_w_gate.astype(BF16), ffn_w_up.astype(BF16), ffn_w_down.astype(BF16)
    mg_b, mu_b, md_b = moe_w_gate.astype(BF16), moe_w_up.astype(BF16), moe_w_down.astype(BF16)
    router_pad = jnp.pad(moe_router.astype(F32), ((0, 0), (0, 0), (0, LANES - N_EXPERTS)))
    router_hi = router_pad.astype(BF16)
    router_lo = (router_pad - router_hi.astype(F32)).astype(BF16)

    x = jnp.concatenate([x_prompt.reshape(mp, D_MODEL), x_sample.reshape(ms, D_MODEL)], axis=0)
    xb = x.astype(BF16)

    pak, pav, sak, sav = [], [], [], []
    chain = prompt_chain = None
    for l in range(depth):
        qa, kva, qb, kb, vb, g, kat, vat, kbt, vbt = _inproj(xb, w_in_b[l], bp, seq, l, depth, prompt_chain)
        prompt_chain = (kbt, vbt)
        oa_p, ob_p = _attn_prompt(attn_sink[l], qa, qb, kva, kb, vb, tabs, bp, seq)
        oa_s, ob_s, nak, nav, nbk, nbv = _attn_sample(l, depth, attn_sink[l], qa, qb, kva, kb, vb,
                                                      cak, cav, cbk, cbv, biases, mp, nb, nq, chain)
        chain = (nbk, nbv)
        pak.append(kat[:, :, seq - WIN_A:])
        pav.append(vat[:, :, seq - WIN_A:])
        sak.append(nak)
        sav.append(nav)

        ln1 = (ln1_g[l].reshape(1, D_MODEL), ln1_b[l].reshape(1, D_MODEL))
        ln2 = (ln2_g[l].reshape(1, D_MODEL), ln2_b[l].reshape(1, D_MODEL))
        i = l // 2
        if l % 2 == 0:
            x1, x1b = _post(x, oa_p, ob_p, oa_s, ob_s, g, wa_b[l], wb_b[l], wo_b[l], *ln1, alpha)
            x, xb = _ffn(x1, x1b, fg_b[i], fu_b[i], fd_b[i], *ln2, alpha)
        else:
            x1, x1b, comb = _post(x, oa_p, ob_p, oa_s, ob_s, g, wa_b[l], wb_b[l], wo_b[l], *ln1, alpha,
                                  (router_hi[i], router_lo[i]))
            x, xb = _moe(i, x1, x1b, comb, mg_b, mu_b, md_b, *ln2, alpha)

    return (x[:mp].reshape(bp, seq, D_MODEL), x[mp:].reshape(nb, nq, D_MODEL),
            _seq_major(jnp.stack(pak), KV_A), _seq_major(jnp.stack(pav), KV_A),
            _seq_major(prompt_chain[0], H_B), _seq_major(prompt_chain[1], H_B),
            _seq_major(jnp.stack(sak), KV_A), _seq_major(jnp.stack(sav), KV_A),
            _seq_major(chain[0], H_B), _seq_major(chain[1], H_B))
```

```python
import functools
import math

import jax
import jax.numpy as jnp
from jax import lax
from jax.experimental import pallas as pl
from jax.experimental.pallas import tpu as pltpu

F32 = jnp.float32
BF16 = jnp.bfloat16

D_MODEL = 1024
HEAD_DIM = 64
H_A = 8
KV_A = 2
WIN_A = 128
H_B = 8
DILATED_PAIRS = ((128, 1), (512, 4), (2048, 16))
W_MAX = 2048
N_EXPERTS = 8
LN_EPS = 1e-5
NEG_INF = -1e30
QA_W = H_A * HEAD_DIM
KVA_W = 2 * KV_A * HEAD_DIM
HB_W = H_B * HEAD_DIM
IN_COLS = QA_W + KVA_W + 3 * HB_W + 2 * D_MODEL
LANES = 128
VMEM_LIMIT_BYTES = 56 * 1024 * 1024


def _params(*sem):
    return pltpu.CompilerParams(dimension_semantics=sem, vmem_limit_bytes=VMEM_LIMIT_BYTES)


def _row_tile(m, cap):
    t = cap
    while m % t:
        t //= 2
    assert t >= 8, (m, cap)
    return t


def _sigmoid(x):
    return 1.0 / (1.0 + jnp.exp(-x))


def _layer_norm(z, g, b):
    mu = jnp.mean(z, axis=-1, keepdims=True)
    zc = z - mu
    var = jnp.mean(zc * zc, axis=-1, keepdims=True)
    return zc * lax.rsqrt(var + LN_EPS) * g + b


def _dot(a, b):
    return jnp.dot(a, b, preferred_element_type=F32)


def _dot_nt(a, b):
    return lax.dot_general(a, b, (((1,), (1,)), ((), ())), preferred_element_type=F32)


def _inproj_kernel(*refs, prompt_tiles, chained):
    x_ref, w_ref = refs[:2]
    (qa_ref, kva_ref, qb_ref, kb_ref, vb_ref, g_ref,
     kat_ref, vat_ref, kbt_ref, vbt_ref) = refs[2 + (2 if chained else 0):]
    x = x_ref[...]
    scale = HEAD_DIM ** -0.5

    def mm(c0, c1):
        return _dot(x, w_ref[:, c0:c1])

    c = 0
    qa_ref[...] = mm(c, c + QA_W) * scale
    c += QA_W
    kva = mm(c, c + KVA_W)
    kva_ref[...] = kva
    c += KVA_W
    qb_ref[...] = mm(c, c + HB_W) * scale
    c += HB_W
    kb = mm(c, c + HB_W)
    kb_ref[...] = kb
    c += HB_W
    vb = mm(c, c + HB_W)
    vb_ref[...] = vb
    c += HB_W
    g_ref[...] = mm(c, c + 2 * D_MODEL)

    @pl.when(pl.program_id(0) < prompt_tiles)
    def _seq_minor_outputs():
        ga = KV_A * HEAD_DIM
        kat_ref[...] = kva[:, 0:ga].T
        vat_ref[...] = kva[:, ga:2 * ga].T
        kbt_ref[...] = kb.T
        vbt_ref[...] = vb.T


def _inproj(xb, w, bp, seq, layer, depth, chain):
    m = xb.shape[0]
    tm = _row_tile(math.gcd(m, seq), 512)
    tiles_per_seq = seq // tm
    prompt_tiles = bp * tiles_per_seq
    widths = (QA_W, KVA_W, HB_W, HB_W, HB_W, 2 * D_MODEL)
    ga = KV_A * HEAD_DIM

    def seq_tile(i):
        j = jnp.minimum(i, prompt_tiles - 1)
        return j // tiles_per_seq, j % tiles_per_seq

    def per_layer(rows):
        def index(i):
            b, j = seq_tile(i)
            return (b, 0, j)
        return pl.BlockSpec((None, rows, tm), index)

    def stacked(rows):
        def index(i):
            b, j = seq_tile(i)
            return (layer, b, 0, j)
        return pl.BlockSpec((None, None, rows, tm), index)

    in_specs = [pl.BlockSpec((tm, D_MODEL), lambda i: (i, 0)),
                pl.BlockSpec((D_MODEL, IN_COLS), lambda i: (0, 0))]
    args = [xb, w]
    aliases = {}
    if chain is not None:
        in_specs += [pl.BlockSpec(memory_space=pl.ANY)] * 2
        aliases = {2: 8, 3: 9}
        args += list(chain)
    return pl.pallas_call(
        functools.partial(_inproj_kernel, prompt_tiles=prompt_tiles, chained=chain is not None),
        grid=(m // tm,),
        in_specs=in_specs,
        out_specs=[pl.BlockSpec((tm, wd), lambda i: (i, 0)) for wd in widths]
        + [per_layer(ga), per_layer(ga), stacked(HB_W), stacked(HB_W)],
        out_shape=[jax.ShapeDtypeStruct((m, wd), F32) for wd in widths]
        + [jax.ShapeDtypeStruct((bp, ga, seq), F32)] * 2 + [jax.ShapeDtypeStruct((depth, bp, HB_W, seq), F32)] * 2,
        input_output_aliases=aliases,
        compiler_params=_params("arbitrary"),
        name="inproj",
    )(*args)


BLK = 128
HEADS_PER_STEP = 2


def _tile_bias(slopes, n_q, n_k, shift, dist_scale, max_sub):
    r = jnp.arange(n_q, dtype=jnp.int32)[:, None]
    c = jnp.arange(n_k, dtype=jnp.int32)[None, :]
    dsub = shift + r - c
    ok = (dsub >= 0) & (dsub <= max_sub)
    b = -slopes[:, None, None] * (dist_scale * dsub).astype(F32)[None]
    full = jnp.where(ok[None], b, NEG_INF)
    first = jnp.where((ok & (c >= shift))[None], b, NEG_INF)
    return jnp.stack([full, first], axis=1).astype(F32)


def _pair_tile(q_s, rows, k, v, bias_ref, first, sinks=None):
    low = lax.broadcasted_iota(jnp.int32, (1, HEADS_PER_STEP * HEAD_DIM), 1) < HEAD_DIM
    pv, inv, lse = [], [], []
    for hh in range(HEADS_PER_STEP):
        s = _dot_nt(q_s[hh, rows, :], k) + bias_ref[hh, first]
        m = jnp.max(s, axis=-1, keepdims=True)
        if sinks is not None:
            m = jnp.maximum(m, sinks[hh])
        p = jnp.exp(s - m)
        l = jnp.sum(p, axis=-1, keepdims=True)
        if sinks is not None:
            l = l + jnp.exp(sinks[hh] - m)
        pv.append(_dot(p.astype(BF16), v))
        inv.append(1.0 / l)
        lse.append(m + jnp.log(l))
    return jnp.where(low, pv[0], pv[1]) * jnp.where(low, inv[0], inv[1]), jnp.where(low, lse[0], lse[1])


def _attn_prompt_kernel(sink_ref, qa_ref, qb_ref, kva_ref, kb_ref, vb_ref, ta_ref, t1_ref, t2_ref, t3_ref,
                        oa_ref, ob_ref,
                        qa_s, ka_s, va_s, q1_s, k1_s, v1_s, q4_s, k4_s, v4_s, q16_s, k16_s, v16_s, ro_s, rl_s,
                        *, seq):
    hp = pl.program_id(1)
    nblk = seq // BLK
    n4, n16 = seq // 4, seq // 16
    sub4 = n4 + BLK
    hd = HEAD_DIM
    low = lax.broadcasted_iota(jnp.int32, (1, HEADS_PER_STEP * hd), 1) < hd

    @pl.when((pl.program_id(0) == 0) & (hp == 0))
    def _zero_key_pads():
        for ref in (ka_s, va_s, k1_s, v1_s, k4_s, v4_s):
            ref[...] = jnp.zeros(ref.shape, BF16)

    def put_queries(q_s, dst, q):
        q_s[0, dst, :] = jnp.where(low, q, 0.0).astype(BF16)
        q_s[1, dst, :] = jnp.where(low, 0.0, q).astype(BF16)

    group0 = (hp // (H_A // KV_A // HEADS_PER_STEP)) == 0

    def both_halves(x):
        return jnp.where(low == group0, x, pltpu.roll(x, hd, axis=1)).astype(BF16)

    def stage_natural(c, carry):
        r0 = pl.multiple_of(c * BLK, BLK)
        rows, prows = pl.ds(r0, BLK), pl.ds(r0 + BLK, BLK)
        kva = kva_ref[rows, :]
        ka_s[prows, :] = both_halves(kva[:, 0:2 * hd])
        va_s[prows, :] = both_halves(kva[:, 2 * hd:4 * hd])
        put_queries(qa_s, rows, qa_ref[rows, :])
        put_queries(q1_s, rows, qb_ref[rows, :])
        k1_s[prows, :] = kb_ref[rows, :].astype(BF16)
        v1_s[prows, :] = vb_ref[rows, :].astype(BF16)
        return carry
    lax.fori_loop(0, nblk, stage_natural, 0, unroll=4)

    for r4 in range(4):
        def stage_mod4(c, carry, r4=r4):
            j0 = pl.multiple_of(c * BLK, BLK)
            src = pl.ds(r4 + 4 * j0, BLK, stride=4)
            put_queries(q4_s, pl.ds(r4 * n4 + j0, BLK), qb_ref[src, :])
            k4_s[pl.ds(r4 * sub4 + BLK + j0, BLK), :] = kb_ref[src, :].astype(BF16)
            v4_s[pl.ds(r4 * sub4 + BLK + j0, BLK), :] = vb_ref[src, :].astype(BF16)
            return carry
        lax.fori_loop(0, n4 // BLK, stage_mod4, 0, unroll=True)

    def stage_mod16(r, carry):
        src = pl.ds(r, n16, stride=16)
        dst = pl.ds(pl.multiple_of(r * n16, n16), n16)
        put_queries(q16_s, dst, qb_ref[src, :])
        k16_s[dst, :] = kb_ref[src, :].astype(BF16)
        v16_s[dst, :] = vb_ref[src, :].astype(BF16)
        return carry
    lax.fori_loop(0, 16, stage_mod16, 0, unroll=4)

    sinks = [sink_ref[HEADS_PER_STEP * hp + hh] for hh in range(HEADS_PER_STEP)]

    def mixer_a(i, carry):
        r0 = pl.multiple_of(i * BLK, BLK)
        o, _ = _pair_tile(qa_s, pl.ds(r0, BLK), ka_s[pl.ds(r0, 2 * BLK), :], va_s[pl.ds(r0, 2 * BLK), :],
                          ta_ref, jnp.where(i == 0, 1, 0), sinks)
        oa_ref[pl.ds(r0, BLK), :] = o.astype(BF16)
        return carry
    lax.fori_loop(0, nblk, mixer_a, 0, unroll=True)

    def keep(pair, dst, o_lse):
        ro_s[pair, dst, :] = o_lse[0]
        rl_s[pair, dst, :] = jnp.broadcast_to(o_lse[1], o_lse[0].shape)

    def pair_w128(i, carry):
        r0 = pl.multiple_of(i * BLK, BLK)
        keep(0, pl.ds(r0, BLK), _pair_tile(q1_s, pl.ds(r0, BLK), k1_s[pl.ds(r0, 2 * BLK), :],
                                          v1_s[pl.ds(r0, 2 * BLK), :], t1_ref, jnp.where(i == 0, 1, 0)))
        return carry
    lax.fori_loop(0, nblk, pair_w128, 0, unroll=True)

    for r4 in range(4):
        def pair_w512(i, carry, r4=r4):
            j0 = pl.multiple_of(i * BLK, BLK)
            keep(1, pl.ds(r4 + 4 * j0, BLK, stride=4),
                 _pair_tile(q4_s, pl.ds(r4 * n4 + j0, BLK), k4_s[pl.ds(r4 * sub4 + j0, 2 * BLK), :],
                            v4_s[pl.ds(r4 * sub4 + j0, 2 * BLK), :], t2_ref, jnp.where(i == 0, 1, 0)))
            return carry
        lax.fori_loop(0, n4 // BLK, pair_w512, 0, unroll=True)

    def pair_w2048(r, carry):
        src = pl.ds(pl.multiple_of(r * n16, n16), n16)
        keep(2, pl.ds(r, n16, stride=16), _pair_tile(q16_s, src, k16_s[src, :], v16_s[src, :], t3_ref, 0))
        return carry
    lax.fori_loop(0, 16, pair_w2048, 0, unroll=True)

    def merge(i, carry):
        rows = pl.ds(pl.multiple_of(i * BLK, BLK), BLK)
        l0, l1, l2 = rl_s[0, rows, :], rl_s[1, rows, :], rl_s[2, rows, :]
        mx = jnp.maximum(l0, jnp.maximum(l1, l2))
        w0, w1, w2 = jnp.exp(l0 - mx), jnp.exp(l1 - mx), jnp.exp(l2 - mx)
        o = (w0 * ro_s[0, rows, :] + w1 * ro_s[1, rows, :] + w2 * ro_s[2, rows, :]) / (w0 + w1 + w2)
        ob_ref[rows, :] = o.astype(BF16)
        return carry
    lax.fori_loop(0, nblk, merge, 0, unroll=4)


def _attn_prompt(sink, qa, qb, kva, kb, vb, tabs, batch, seq):
    mp = batch * seq
    n4 = seq // 4
    lanes = HEADS_PER_STEP * HEAD_DIM
    smem = pl.BlockSpec(memory_space=pltpu.SMEM)
    cols = pl.BlockSpec((seq, lanes), lambda b, hp: (b, hp))

    def tab(t):
        return pl.BlockSpec((HEADS_PER_STEP,) + t.shape[1:], lambda b, hp: (hp, 0, 0, 0))

    def masked_q():
        return pltpu.VMEM((HEADS_PER_STEP, seq, lanes), BF16)

    def packed(rows):
        return pltpu.VMEM((rows, lanes), BF16)

    return pl.pallas_call(
        functools.partial(_attn_prompt_kernel, seq=seq),
        grid=(batch, H_B // HEADS_PER_STEP),
        in_specs=[smem, cols, cols, pl.BlockSpec((seq, KVA_W), lambda b, hp: (b, 0)), cols, cols]
        + [tab(t) for t in tabs],
        out_specs=[cols, cols],
        out_shape=[jax.ShapeDtypeStruct((mp, QA_W), BF16), jax.ShapeDtypeStruct((mp, HB_W), BF16)],
        scratch_shapes=[masked_q(), packed(seq + BLK), packed(seq + BLK),
                        masked_q(), packed(seq + BLK), packed(seq + BLK),
                        masked_q(), packed(4 * (n4 + BLK)), packed(4 * (n4 + BLK)),
                        masked_q(), packed(seq), packed(seq),
                        pltpu.VMEM((3, seq, lanes), F32), pltpu.VMEM((3, seq, lanes), F32)],
        compiler_params=_params("arbitrary", "arbitrary"),
        name="attn_prompt",
    )(sink, qa, qb, kva, kb, vb, *tabs)


def _sample_bias(slopes, nq, cache_len, max_dist, dilated):
    i = jnp.arange(nq, dtype=jnp.int32)[:, None]

    def make(dist, real):
        if dilated:
            cnt = jnp.zeros(dist.shape, jnp.int32)
            for w, d in DILATED_PAIRS:
                cnt = cnt + jnp.where(((dist & (d - 1)) == 0) & (dist <= w), 1, 0)
            ok = real & (dist >= 0) & (cnt > 0)
            extra = jnp.where(cnt == 3, math.log(3.0), jnp.where(cnt == 2, math.log(2.0), 0.0)).astype(F32)
        else:
            ok = real & (dist >= 0) & (dist <= max_dist)
            extra = jnp.zeros(dist.shape, F32)
        bias = -slopes[:, None, None] * dist.astype(F32)[None] + extra[None]
        return jnp.where(ok[None], bias, NEG_INF).reshape(slopes.shape[0] * nq, dist.shape[1]).astype(F32)

    key = jnp.arange(cache_len, dtype=jnp.int32)[None, :]
    new = jnp.arange(LANES, dtype=jnp.int32)[None, :] - (LANES - nq)
    return make(cache_len + i - key, key >= 0), make(i - new, new >= 0)


CACHE_RING = 3


def _attn_sample_kernel(*refs, nq, la, lb, chained, layer, nb):
    (sink_ref, qa_ref, qb_ref, kva_ref, kb_ref, vb_ref, cak_ref, cav_ref, cbk_hbm, cbv_hbm,
     bac_ref, ban_ref, bbc_ref, bbn_ref) = refs[:14]
    refs = refs[14 + (2 if chained else 0):]
    oa_ref, ob_ref, nak_ref, nav_ref, nbk_ref, nbv_ref, s_s, p_s, kbuf, vbuf, sems = refs
    b = pl.program_id(0)

    def fetch(step, slot):
        return (pltpu.make_async_copy(cbk_hbm.at[layer, step], kbuf.at[slot], sems.at[0, slot]),
                pltpu.make_async_copy(cbv_hbm.at[layer, step], vbuf.at[slot], sems.at[1, slot]))

    @pl.when(b == 0)
    def _prime():
        for step in range(min(CACHE_RING - 1, nb)):
            for cp in fetch(step, step):
                cp.start()

    ahead = b + (CACHE_RING - 1)

    @pl.when(ahead < nb)
    def _prefetch():
        for cp in fetch(ahead, lax.rem(ahead, CACHE_RING)):
            cp.start()

    slot = lax.rem(b, CACHE_RING)
    for cp in fetch(b, slot):
        cp.wait()
    cbk_ref, cbv_ref = kbuf.at[slot], vbuf.at[slot]
    rows = H_A * nq
    hd = HEAD_DIM
    ga = KV_A * hd
    rep = H_A // KV_A

    def new_rows(x):
        return jnp.concatenate([jnp.zeros((LANES - nq, x.shape[1]), F32), x], axis=0)

    lane = lax.broadcasted_iota(jnp.int32, (1, LANES), 1)

    def shifted(cache_ref, out_ref, new_t, length, row_chunk):
        for r0 in range(0, cache_ref.shape[0], row_chunk):
            rs = slice(r0, r0 + row_chunk)
            rolled = pltpu.roll(cache_ref[rs, :], length - nq, axis=1)
            if length > LANES:
                out_ref[rs, 0:length - LANES] = rolled[:, 0:length - LANES]
            out_ref[rs, length - LANES:length] = jnp.where(lane < LANES - nq, rolled[:, length - LANES:length],
                                                           new_t[rs, :])

    qa = qa_ref[...]
    kva = kva_ref[...]
    zeros = jnp.zeros((nq, hd), F32)
    qbd_a = jnp.concatenate(
        [jnp.concatenate([qa[:, h * hd:(h + 1) * hd] if g == h // rep else zeros for g in range(KV_A)], axis=1)
         for h in range(H_A)], axis=0).astype(BF16)
    kn_a = new_rows(kva[:, 0:ga])
    vn_a = new_rows(kva[:, ga:2 * ga])
    hrow = lax.broadcasted_iota(jnp.int32, (rows, 1), 0) // nq
    sink = jnp.zeros((rows, 1), F32)
    for h in range(H_A):
        sink = jnp.where(hrow == h, sink_ref[h], sink)
    s_c = _dot(qbd_a, cak_ref[...].astype(BF16)) + bac_ref[...]
    s_n = _dot_nt(qbd_a, kn_a.astype(BF16)) + ban_ref[...]
    m = jnp.maximum(jnp.maximum(jnp.max(s_c, axis=-1, keepdims=True), jnp.max(s_n, axis=-1, keepdims=True)), sink)
    p_c, p_n = jnp.exp(s_c - m), jnp.exp(s_n - m)
    l = jnp.sum(p_c, axis=-1, keepdims=True) + jnp.sum(p_n, axis=-1, keepdims=True) + jnp.exp(sink - m)
    o = (_dot_nt(p_c.astype(BF16), cav_ref[...].astype(BF16)) + _dot(p_n.astype(BF16), vn_a.astype(BF16))) / l
    oa_ref[...] = jnp.concatenate(
        [o[h * nq:(h + 1) * nq, (h // rep) * hd:(h // rep + 1) * hd] for h in range(H_A)], axis=1)
    shifted(cak_ref, nak_ref, kn_a.T, la, ga)
    shifted(cav_ref, nav_ref, vn_a.T, la, ga)

    r = lax.broadcasted_iota(jnp.int32, (rows, HB_W), 0)
    c = lax.broadcasted_iota(jnp.int32, (rows, HB_W), 1)
    own = (r // nq) == (c // hd)
    qbd = jnp.where(own, jnp.concatenate([qb_ref[...]] * H_B, axis=0), 0.0).astype(BF16)
    kn_b = new_rows(kb_ref[...])
    vn_b = new_rows(vb_ref[...])
    chunk = 512
    for c0 in range(0, lb, chunk):
        cs = slice(c0, c0 + chunk)
        s_s[:, cs] = _dot(qbd, cbk_ref[:, cs].astype(BF16)) + bbc_ref[:, cs]
    s_n = _dot_nt(qbd, kn_b.astype(BF16)) + bbn_ref[...]
    s_c = s_s[...]
    m = jnp.maximum(jnp.max(s_c, axis=-1, keepdims=True), jnp.max(s_n, axis=-1, keepdims=True))
    p_c, p_n = jnp.exp(s_c - m), jnp.exp(s_n - m)
    l = jnp.sum(p_c, axis=-1, keepdims=True) + jnp.sum(p_n, axis=-1, keepdims=True)
    p_s[...] = p_c.astype(BF16)
    o = _dot(p_n.astype(BF16), vn_b.astype(BF16))
    for c0 in range(0, lb, chunk):
        cs = slice(c0, c0 + chunk)
        o = o + _dot_nt(p_s[:, cs], cbv_ref[:, cs].astype(BF16))
    o = jnp.where(own, o, 0.0) / l
    out = o[0:nq]
    for h in range(1, H_B):
        out = out + o[h * nq:(h + 1) * nq]
    ob_ref[...] = out
    shifted(cbk_ref, nbk_ref, kn_b.T, lb, 64)
    shifted(cbv_ref, nbv_ref, vn_b.T, lb, 64)


def _attn_sample(layer, depth, sink, qa, qb, kva, kb, vb, cak, cav, cbk, cbv, biases, row0, nb, nq, chain):
    la, lb = cak.shape[3], cbk.shape[3]
    ga = KV_A * HEAD_DIM
    rows = H_A * nq
    blk0 = row0 // nq
    smem = pl.BlockSpec(memory_space=pltpu.SMEM)

    def tok(wd):
        return pl.BlockSpec((nq, wd), lambda b: (blk0 + b, 0))

    def cache(r, length):
        return pl.BlockSpec((None, None, r, length), lambda b: (layer, b, 0, 0))

    def const(a):
        return pl.BlockSpec(a.shape, lambda b: (0, 0))

    hbm = pl.BlockSpec(memory_space=pl.ANY)
    in_specs = [smem, tok(QA_W), tok(HB_W), tok(KVA_W), tok(HB_W), tok(HB_W),
                cache(ga, la), cache(ga, la), hbm, hbm] + [const(a) for a in biases]
    args = [sink, qa, qb, kva, kb, vb, cak, cav, cbk, cbv, *biases]
    aliases = {}
    if chain is not None:
        in_specs += [pl.BlockSpec(memory_space=pl.ANY)] * 2
        aliases = {len(args): 4, len(args) + 1: 5}
        args += list(chain)
    out_tok = pl.BlockSpec((nq, QA_W), lambda b: (b, 0))
    return pl.pallas_call(
        functools.partial(_attn_sample_kernel, nq=nq, la=la, lb=lb, chained=chain is not None, layer=layer, nb=nb),
        grid=(nb,),
        in_specs=in_specs,
        out_specs=[out_tok, out_tok,
                   pl.BlockSpec((None, ga, la), lambda b: (b, 0, 0)), pl.BlockSpec((None, ga, la), lambda b: (b, 0, 0)),
                   cache(HB_W, lb), cache(HB_W, lb)],
        out_shape=[jax.ShapeDtypeStruct((nb * nq, QA_W), F32), jax.ShapeDtypeStruct((nb * nq, HB_W), F32),
                   jax.ShapeDtypeStruct((nb, ga, la), F32), jax.ShapeDtypeStruct((nb, ga, la), F32),
                   jax.ShapeDtypeStruct((depth, nb, HB_W, lb), F32), jax.ShapeDtypeStruct((depth, nb, HB_W, lb), F32)],
        scratch_shapes=[pltpu.VMEM((rows, lb), F32), pltpu.VMEM((rows, lb), BF16),
                        pltpu.VMEM((CACHE_RING, HB_W, lb), F32), pltpu.VMEM((CACHE_RING, HB_W, lb), F32),
                        pltpu.SemaphoreType.DMA((2, CACHE_RING))],
        input_output_aliases=aliases,
        compiler_params=_params("arbitrary"),
        name="attn_sample",
    )(*args)


def _post_kernel(*refs, alpha, with_router, prompt_tiles):
    if with_router:
        (x_ref, oap_ref, obp_ref, oas_ref, obs_ref, g_ref, wa_ref, wb_ref, wo_ref, lg_ref, lb_ref, wrh_ref, wrl_ref,
         x1_ref, x1b_ref, comb_ref) = refs
    else:
        (x_ref, oap_ref, obp_ref, oas_ref, obs_ref, g_ref, wa_ref, wb_ref, wo_ref, lg_ref, lb_ref,
         x1_ref, x1b_ref) = refs
    prompt = pl.program_id(0) < prompt_tiles
    oa = jnp.where(prompt, oap_ref[...], oas_ref[...].astype(BF16))
    ob = jnp.where(prompt, obp_ref[...], obs_ref[...].astype(BF16))
    a = _dot(oa, wa_ref[...])
    b = _dot(ob, wb_ref[...])
    g = g_ref[...]
    merged = _sigmoid(g[:, :D_MODEL]) * a + _sigmoid(g[:, D_MODEL:]) * b
    y = _dot(merged.astype(BF16), wo_ref[...])
    x1 = _layer_norm(alpha * x_ref[...] + y, lg_ref[...], lb_ref[...])
    x1b = x1.astype(BF16)
    x1_ref[...] = x1
    x1b_ref[...] = x1b
    if with_router:
        x1l = (x1 - x1b.astype(F32)).astype(BF16)
        logits = _dot(x1b, wrh_ref[...]) + (_dot(x1l, wrh_ref[...]) + _dot(x1b, wrl_ref[...]))
        lane = lax.broadcasted_iota(jnp.int32, logits.shape, 1)
        logits = jnp.where(lane < N_EXPERTS, logits, -jnp.inf)
        t1 = jnp.max(logits, axis=-1, keepdims=True)
        lane = lane.astype(F32)
        i1 = jnp.min(jnp.where(logits == t1, lane, float(LANES)), axis=-1, keepdims=True)
        rest = jnp.where(lane == i1, -jnp.inf, logits)
        t2 = jnp.max(rest, axis=-1, keepdims=True)
        i2 = jnp.min(jnp.where(rest == t2, lane, float(LANES)), axis=-1, keepdims=True)
        e2 = jnp.exp(t2 - t1)
        den = 1.0 + e2
        comb_ref[...] = jnp.where(lane == i1, 1.0 / den, jnp.where(lane == i2, e2 / den, 0.0))


def _post(x, oa_p, ob_p, oa_s, ob_s, g, wa, wb, wo, ln_g, ln_b, alpha, w_router=None):
    m = x.shape[0]
    tm = _row_tile(math.gcd(oa_p.shape[0], oa_s.shape[0]), 512)
    prompt_tiles = oa_p.shape[0] // tm
    with_router = w_router is not None

    def rows(wd):
        return pl.BlockSpec((tm, wd), lambda i: (i, 0))

    def prompt_rows(wd):
        return pl.BlockSpec((tm, wd), lambda i: (jnp.minimum(i, prompt_tiles - 1), 0))

    def sample_rows(wd):
        return pl.BlockSpec((tm, wd), lambda i: (jnp.maximum(i - prompt_tiles, 0), 0))

    def const(shape):
        return pl.BlockSpec(shape, lambda i: (0, 0))

    in_specs = [rows(D_MODEL), prompt_rows(QA_W), prompt_rows(HB_W), sample_rows(QA_W), sample_rows(HB_W),
                rows(2 * D_MODEL), const(wa.shape), const(wb.shape), const(wo.shape),
                const((1, D_MODEL)), const((1, D_MODEL))]
    args = [x, oa_p, ob_p, oa_s, ob_s, g, wa, wb, wo, ln_g, ln_b]
    out_specs = [rows(D_MODEL), rows(D_MODEL)]
    out_shape = [jax.ShapeDtypeStruct((m, D_MODEL), F32), jax.ShapeDtypeStruct((m, D_MODEL), BF16)]
    if with_router:
        in_specs += [const(w_router[0].shape), const(w_router[1].shape)]
        args += list(w_router)
        out_specs.append(rows(LANES))
        out_shape.append(jax.ShapeDtypeStruct((m, LANES), F32))
    return pl.pallas_call(
        functools.partial(_post_kernel, alpha=alpha, with_router=with_router, prompt_tiles=prompt_tiles),
        grid=(m // tm,),
        in_specs=in_specs, out_specs=out_specs, out_shape=out_shape,
        compiler_params=_params("arbitrary"),
        name="post_router" if with_router else "post",
    )(*args)


def _swiglu_chunk(xb, wg, wu, wd):
    hg = _dot(xb, wg)
    hu = _dot(xb, wu)
    return _dot((hg * _sigmoid(hg) * hu).astype(BF16), wd)


def _ffn_kernel(x_ref, xb_ref, wg_ref, wu_ref, wd_ref, lg_ref, lb_ref, x2_ref, x2b_ref, acc_ref,
                *, alpha, nf):
    f = pl.program_id(1)

    @pl.when(f == 0)
    def _():
        acc_ref[...] = jnp.zeros(acc_ref.shape, F32)

    acc_ref[...] += _swiglu_chunk(xb_ref[...], wg_ref[...], wu_ref[...], wd_ref[...])

    @pl.when(f == nf - 1)
    def _():
        x2 = _layer_norm(alpha * x_ref[...] + acc_ref[...], lg_ref[...], lb_ref[...])
        x2_ref[...] = x2
        x2b_ref[...] = x2.astype(BF16)


def _ffn(x, xb, wg, wu, wd, ln_g, ln_b, alpha):
    m = x.shape[0]
    tm = _row_tile(m, 512)
    d_ff = wg.shape[1]
    nf = 2
    tf = d_ff // nf
    rows = pl.BlockSpec((tm, D_MODEL), lambda i, f: (i, 0))
    vec = pl.BlockSpec((1, D_MODEL), lambda i, f: (0, 0))
    return pl.pallas_call(
        functools.partial(_ffn_kernel, alpha=alpha, nf=nf),
        grid=(m // tm, nf),
        in_specs=[rows, rows,
                  pl.BlockSpec((D_MODEL, tf), lambda i, f: (0, f)),
                  pl.BlockSpec((D_MODEL, tf), lambda i, f: (0, f)),
                  pl.BlockSpec((tf, D_MODEL), lambda i, f: (f, 0)),
                  vec, vec],
        out_specs=[rows, rows],
        out_shape=[jax.ShapeDtypeStruct((m, D_MODEL), F32), jax.ShapeDtypeStruct((m, D_MODEL), BF16)],
        scratch_shapes=[pltpu.VMEM((tm, D_MODEL), F32)],
        compiler_params=_params("arbitrary", "arbitrary"),
        name="ffn",
    )(x, xb, wg, wu, wd, ln_g, ln_b)


MOE_TILE = 512
MOE_SUB = 256
FLAG_FIRST, FLAG_LAST, FLAG_ACTIVE = 1, 2, 4


def _steps(n, lo, total_steps):
    end = jnp.cumsum(n)
    start = end - n
    s = jnp.arange(total_steps, dtype=jnp.int32)
    sc = jnp.minimum(s, end[-1] - 1)
    g = jnp.searchsorted(end, sc, side="right", method="compare_all").astype(jnp.int32)
    return g, (lo[g] + sc - start[g]).astype(jnp.int32), sc, s < end[-1], start, end


def _step_flags(active, first, last):
    return (jnp.where(active & first, FLAG_FIRST, 0) + jnp.where(active & last, FLAG_LAST, 0)
            + jnp.where(active, FLAG_ACTIVE, 0)).astype(jnp.int32)


def _route_plan(comb, tm):
    m = comb.shape[0]
    onehot = comb[:, :N_EXPERTS] > 0.0
    oh = onehot.astype(jnp.int32)
    csum = jnp.cumsum(oh, axis=0)
    cnt = csum[-1]
    tiles_e = (cnt + tm - 1) // tm
    tile_end = jnp.cumsum(tiles_e)
    tile_off = tile_end - tiles_e
    pos = jnp.where(onehot, (tile_off * tm)[None, :] + csum - oh, -1)
    nt = 2 * m // tm + N_EXPERTS
    n_active = tile_end[-1].astype(jnp.int32)
    tiles = jnp.arange(nt, dtype=jnp.int32)
    tile_expert = jnp.minimum(jnp.searchsorted(tile_end, tiles, side="right", method="compare_all"),
                              N_EXPERTS - 1).astype(jnp.int32)

    ts = MOE_SUB
    per = tm // ts
    subs = jnp.arange(nt * per, dtype=jnp.int32)
    sub_expert = tile_expert[subs // per]
    k0 = (subs - tile_off[sub_expert] * per) * ts
    k1 = jnp.minimum(k0 + ts, cnt[sub_expert]) - 1
    find = jax.vmap(lambda col, q: jnp.searchsorted(col, q, side="left", method="compare_all"))

    def token_of_rank(q):
        every = find(csum.T, jnp.broadcast_to(q[None, :], (N_EXPERTS, nt * per)))
        return jnp.take_along_axis(every, sub_expert[None, :], axis=0)[0]

    t_lo, t_hi = token_of_rank(k0 + 1), token_of_rank(k1 + 1)
    has_rows = k1 >= k0
    c_lo = jnp.where(has_rows, t_lo // tm, 0).astype(jnp.int32)
    n = jnp.where(subs < n_active * per, jnp.where(has_rows, t_hi // tm - t_lo // tm + 1, 1), 0).astype(jnp.int32)
    g, chunk, sc, active, start, end = _steps(n, c_lo, nt * per + N_EXPERTS * (m // tm))
    dispatch = (g, sub_expert[g], chunk, _step_flags(active, sc == start[g], sc == end[g] - 1))

    nti = m // tm
    posr = pos.reshape(nti, tm, N_EXPERTS)
    p_hi = jnp.max(posr, axis=1)
    p_lo = jnp.min(jnp.where(posr >= 0, posr, nt * tm), axis=1)
    c_lo = (p_lo // ts).reshape(-1).astype(jnp.int32)
    n = jnp.where(p_hi >= 0, p_hi // ts - p_lo // ts + 1, 0).reshape(-1).astype(jnp.int32)
    g, chunk, sc, active, start, end = _steps(n, c_lo, nti * N_EXPERTS + nt * per)
    ti, ex = g // N_EXPERTS, g % N_EXPERTS
    t_start = start.reshape(nti, N_EXPERTS)[:, 0]
    t_end = end.reshape(nti, N_EXPERTS)[:, -1]
    combine = (ti.astype(jnp.int32), ex.astype(jnp.int32), chunk,
               _step_flags(active, sc == t_start[ti], sc == t_end[ti] - 1))
    return pos.astype(F32), tile_expert, n_active.reshape(1), dispatch, combine


def _accumulate(flags, acc_ref, v):
    @pl.when((flags & FLAG_FIRST) != 0)
    def _():
        acc_ref[...] = v

    @pl.when((flags & FLAG_FIRST) == 0)
    def _():
        acc_ref[...] += v


def _dispatch_kernel(tile_ref, ex_ref, chunk_ref, flag_ref, pos_ref, x_ref, o_ref, acc_ref):
    s = pl.program_id(0)
    flags = flag_ref[s]

    @pl.when((flags & FLAG_ACTIVE) != 0)
    def _():
        ts, tc = acc_ref.shape[0], x_ref.shape[0]
        row = (lax.broadcasted_iota(jnp.int32, (ts, tc), 0) + tile_ref[s] * ts).astype(F32)
        pick = jnp.where(pos_ref[pl.ds(ex_ref[s], 1), :] == row, 1.0, 0.0).astype(BF16)
        _accumulate(flags, acc_ref, _dot(pick, x_ref[...]))

        @pl.when((flags & FLAG_LAST) != 0)
        def _():
            o_ref[...] = acc_ref[...].astype(BF16)


def _dispatch(xb, pos_t, plan, n_tiles, tm):
    tile, ex, chunk, flags = plan
    return pl.pallas_call(
        _dispatch_kernel,
        grid_spec=pltpu.PrefetchScalarGridSpec(
            num_scalar_prefetch=4,
            grid=(tile.shape[0],),
            in_specs=[pl.BlockSpec((N_EXPERTS, tm), lambda s, t, e, c, f: (0, c[s])),
                      pl.BlockSpec((tm, D_MODEL), lambda s, t, e, c, f: (c[s], 0))],
            out_specs=pl.BlockSpec((MOE_SUB, D_MODEL), lambda s, t, e, c, f: (t[s], 0)),
            scratch_shapes=[pltpu.VMEM((MOE_SUB, D_MODEL), F32)]),
        out_shape=jax.ShapeDtypeStruct((n_tiles * tm, D_MODEL), BF16),
        compiler_params=_params("arbitrary"),
        name="moe_dispatch",
    )(tile, ex, chunk, flags, pos_t, xb)


def _experts_kernel(texp_ref, nact_ref, xs_ref, wg_ref, wu_ref, wd_ref, ys_ref, acc_ref, *, nf):
    j = pl.program_id(0)
    f = pl.program_id(1)

    @pl.when(j < nact_ref[0])
    def _():
        @pl.when(f == 0)
        def _():
            acc_ref[...] = jnp.zeros(acc_ref.shape, F32)

        acc_ref[...] += _swiglu_chunk(xs_ref[...], wg_ref[...], wu_ref[...], wd_ref[...])

        @pl.when(f == nf - 1)
        def _():
            ys_ref[...] = acc_ref[...].astype(BF16)


def _experts(layer_moe, xs, tile_expert, n_active, wg, wu, wd, tm):
    nt = xs.shape[0] // tm
    nf = 2
    tf = wg.shape[3] // nf

    def row_map(j, f, te, na):
        return (jnp.minimum(j, na[0] - 1), 0)

    def col_f(j, f, na):
        return jnp.where(j < na[0], f, nf - 1)

    def expert(j, te, na):
        return te[jnp.minimum(j, na[0] - 1)]

    return pl.pallas_call(
        functools.partial(_experts_kernel, nf=nf),
        grid_spec=pltpu.PrefetchScalarGridSpec(
            num_scalar_prefetch=2,
            grid=(nt, nf),
            in_specs=[pl.BlockSpec((tm, D_MODEL), row_map),
                      pl.BlockSpec((None, None, D_MODEL, tf),
                                   lambda j, f, te, na: (layer_moe, expert(j, te, na), 0, col_f(j, f, na))),
                      pl.BlockSpec((None, None, D_MODEL, tf),
                                   lambda j, f, te, na: (layer_moe, expert(j, te, na), 0, col_f(j, f, na))),
                      pl.BlockSpec((None, None, tf, D_MODEL),
                                   lambda j, f, te, na: (layer_moe, expert(j, te, na), col_f(j, f, na), 0))],
            out_specs=pl.BlockSpec((tm, D_MODEL), row_map),
            scratch_shapes=[pltpu.VMEM((tm, D_MODEL), F32)]),
        out_shape=jax.ShapeDtypeStruct(xs.shape, BF16),
        compiler_params=_params("arbitrary", "arbitrary"),
        name="moe_experts",
    )(tile_expert, n_active, xs, wg, wu, wd)


def _combine_kernel(ti_ref, ex_ref, chunk_ref, flag_ref, x_ref, pos_ref, comb_ref, ys_ref, lg_ref, lb_ref,
                    x2_ref, x2b_ref, acc_ref, *, alpha):
    s = pl.program_id(0)
    flags = flag_ref[s]

    @pl.when((flags & FLAG_ACTIVE) != 0)
    def _():
        tt, tc = acc_ref.shape[0], ys_ref.shape[0]
        lane = lax.broadcasted_iota(jnp.int32, (tt, LANES), 1)
        mine = lane == ex_ref[s]
        row = jnp.max(jnp.where(mine, pos_ref[...], -1.0), axis=-1, keepdims=True)
        gate = jnp.sum(jnp.where(mine, comb_ref[...], 0.0), axis=-1, keepdims=True)
        col = (lax.broadcasted_iota(jnp.int32, (tt, tc), 1) + chunk_ref[s] * tc).astype(F32)
        pick = jnp.where(row == col, 1.0, 0.0).astype(BF16)
        _accumulate(flags, acc_ref, gate * _dot(pick, ys_ref[...]))

        @pl.when((flags & FLAG_LAST) != 0)
        def _():
            x2 = _layer_norm(alpha * x_ref[...] + acc_ref[...], lg_ref[...], lb_ref[...])
            x2_ref[...] = x2
            x2b_ref[...] = x2.astype(BF16)


def _combine(x, pos, comb, ys, plan, ln_g, ln_b, alpha, tm):
    ti, ex, chunk, flags = plan
    m = x.shape[0]
    pos_pad = jnp.pad(pos, ((0, 0), (0, LANES - N_EXPERTS)), constant_values=-1.0)

    def tok(wd):
        return pl.BlockSpec((tm, wd), lambda s, ti, ex, c, f: (ti[s], 0))

    vec = pl.BlockSpec((1, D_MODEL), lambda s, ti, ex, c, f: (0, 0))
    return pl.pallas_call(
        functools.partial(_combine_kernel, alpha=alpha),
        grid_spec=pltpu.PrefetchScalarGridSpec(
            num_scalar_prefetch=4,
            grid=(ti.shape[0],),
            in_specs=[tok(D_MODEL), tok(LANES), tok(LANES),
                      pl.BlockSpec((MOE_SUB, D_MODEL), lambda s, ti, ex, c, f: (c[s], 0)), vec, vec],
            out_specs=[tok(D_MODEL), tok(D_MODEL)],
            scratch_shapes=[pltpu.VMEM((tm, D_MODEL), F32)]),
        out_shape=[jax.ShapeDtypeStruct((m, D_MODEL), F32), jax.ShapeDtypeStruct((m, D_MODEL), BF16)],
        compiler_params=_params("arbitrary"),
        name="moe_combine",
    )(ti, ex, chunk, flags, x, pos_pad, comb, ys, ln_g, ln_b)


def _moe(layer_moe, x, xb, comb, wg, wu, wd, ln_g, ln_b, alpha):
    tm = MOE_TILE
    assert x.shape[0] % tm == 0
    pos, tile_expert, n_active, dispatch_plan, combine_plan = _route_plan(comb, tm)
    xs = _dispatch(xb, pos.T, dispatch_plan, tile_expert.shape[0], tm)
    ys = _experts(layer_moe, xs, tile_expert, n_active, wg, wu, wd, tm)
    return _combine(x, pos, comb, ys, combine_plan, ln_g, ln_b, alpha, tm)


def _seq_minor(cache):
    d, n, length, h, hd = cache.shape
    return jnp.transpose(cache, (0, 1, 3, 4, 2)).reshape(d, n, h * hd, length)


def _seq_major(x, heads):
    d, n, hw, length = x.shape
    return jnp.transpose(x.reshape(d, n, heads, hw // heads, length), (0, 1, 4, 2, 3))


def kernel(x_prompt, x_sample, cache_a_k, cache_a_v, cache_b_k, cache_b_v, w_in, attn_sink, w_branch_a,
           w_branch_b, w_out, ln1_g, ln1_b, ln2_g, ln2_b, ffn_w_gate, ffn_w_up, ffn_w_down, moe_router,
           moe_w_gate, moe_w_up, moe_w_down):
    depth = w_in.shape[0]
    bp, seq, _ = x_prompt.shape
    nb, nq, _ = x_sample.shape
    la, lb = cache_a_k.shape[2], cache_b_k.shape[2]
    assert seq % 512 == 0 and seq <= W_MAX and la == WIN_A and lb == W_MAX and nq == 8
    mp, ms = bp * seq, nb * nq
    alpha = (2.0 * depth) ** 0.25

    n_heads = H_A + H_B
    slopes = jnp.exp2(-8.0 * jnp.arange(1, n_heads + 1, dtype=F32) / n_heads)
    slopes_a, slopes_b = slopes[0::2], slopes[1::2]
    tabs = (_tile_bias(slopes_a, BLK, 2 * BLK, BLK, 1, WIN_A - 1),
            _tile_bias(slopes_b, BLK, 2 * BLK, BLK, DILATED_PAIRS[0][1], DILATED_PAIRS[0][0] // DILATED_PAIRS[0][1]),
            _tile_bias(slopes_b, BLK, 2 * BLK, BLK, DILATED_PAIRS[1][1], DILATED_PAIRS[1][0] // DILATED_PAIRS[1][1]),
            _tile_bias(slopes_b, seq // 16, seq // 16, 0, DILATED_PAIRS[2][1],
                       DILATED_PAIRS[2][0] // DILATED_PAIRS[2][1]))
    biases = _sample_bias(slopes_a, nq, la, WIN_A - 1, False) + _sample_bias(slopes_b, nq, lb, W_MAX, True)

    cak, cav, cbk, cbv = (_seq_minor(c) for c in (cache_a_k, cache_a_v, cache_b_k, cache_b_v))

    w_in_b = w_in.astype(BF16)
    wa_b, wb_b, wo_b = w_branch_a.astype(BF16), w_branch_b.astype(BF16), w_out.astype(BF16)
    fg_b, fu_b, fd_b = ffn_w_gate.astype(BF16), ffn_w_up.astype(BF16), ffn_w_down.astype(BF16)
    mg_b, mu_b, md_b = moe_w_gate.astype(BF16), moe_w_up.astype(BF16), moe_w_down.astype(BF16)
    router_pad = jnp.pad(moe_router.astype(F32), ((0, 0), (0, 0), (0, LANES - N_EXPERTS)))
    router_hi = router_pad.astype(BF16)
    router_lo = (router_pad - router_hi.astype(F32)).astype(BF16)

    x = jnp.concatenate([x_prompt.reshape(mp, D_MODEL), x_sample.reshape(ms, D_MODEL)], axis=0)
    xb = x.astype(BF16)

    pak, pav, sak, sav = [], [], [], []
    chain = prompt_chain = None
    for l in range(depth):
        qa, kva, qb, kb, vb, g, kat, vat, kbt, vbt = _inproj(xb, w_in_b[l], bp, seq, l, depth, prompt_chain)
        prompt_chain = (kbt, vbt)
        oa_p, ob_p = _attn_prompt(attn_sink[l], qa, qb, kva, kb, vb, tabs, bp, seq)
        oa_s, ob_s, nak, nav, nbk, nbv = _attn_sample(l, depth, attn_sink[l], qa, qb, kva, kb, vb,
                                                      cak, cav, cbk, cbv, biases, mp, nb, nq, chain)
        chain = (nbk, nbv)
        pak.append(kat[:, :, seq - WIN_A:])
        pav.append(vat[:, :, seq - WIN_A:])
        sak.append(nak)
        sav.append(nav)

        ln1 = (ln1_g[l].reshape(1, D_MODEL), ln1_b[l].reshape(1, D_MODEL))
        ln2 = (ln2_g[l].reshape(1, D_MODEL), ln2_b[l].reshape(1, D_MODEL))
        i = l // 2
        if l % 2 == 0:
            x1, x1b = _post(x, oa_p, ob_p, oa_s, ob_s, g, wa_b[l], wb_b[l], wo_b[l], *ln1, alpha)
            x, xb = _ffn(x1, x1b, fg_b[i], fu_b[i], fd_b[i], *ln2, alpha)
        else:
            x1, x1b, comb = _post(x, oa_p, ob_p, oa_s, ob_s, g, wa_b[l], wb_b[l], wo_b[l], *ln1, alpha,
                                  (router_hi[i], router_lo[i]))
            x, xb = _moe(i, x1, x1b, comb, mg_b, mu_b, md_b, *ln2, alpha)

    return (x[:mp].reshape(bp, seq, D_MODEL), x[mp:].reshape(nb, nq, D_MODEL),
            _seq_major(jnp.stack(pak), KV_A), _seq_major(jnp.stack(pav), KV_A),
            _seq_major(prompt_chain[0], H_B), _seq_major(prompt_chain[1], H_B),
            _seq_major(jnp.stack(sak), KV_A), _seq_major(jnp.stack(sav), KV_A),
            _seq_major(chain[0], H_B), _seq_major(chain[1], H_B))
```
